```python
import jax, jax.numpy as jnp
from jax import lax
import numpy as np

D_MODEL = 1024
BATCH = 4
SEQ = 4096
DEPTH = 2
DEC_BATCH = 128
DEC_SEQ = 1
PAST_LEN = 2048
PAGE_SIZE = 128

HEAD_DIM = 64
N_HEADS_A = D_MODEL // HEAD_DIM
N_HEADS_B = D_MODEL // HEAD_DIM
N_KV_B = N_HEADS_B // 4
DILATED = ((128, 1), (512, 4), (2048, 16))
N_DIL = len(DILATED)
BLOCK = 128
ROT_DIM = HEAD_DIM // 4
ROPE_THETA = 500000.0
N_GROUPS = 4
EXPERTS_PER_GROUP = 8
N_EXPERTS = N_GROUPS * EXPERTS_PER_GROUP
TOP_K = 2
D_EXPERT = D_MODEL // 4
N_A_LAYERS = DEPTH // 2
N_B_LAYERS = DEPTH - N_A_LAYERS
SB_BIAS_MIN = 2.0
SB_BIAS_MAX = 8.0
RMS_EPS = 1e-6
NEG_INF = -1e30

kernel_name = "yoco_stickbreak_dilated_hmoe_step"


def rmsnorm(x, g):
    xf = x.astype(jnp.float32)
    y = xf * lax.rsqrt(jnp.mean(xf * xf, axis=-1, keepdims=True) + RMS_EPS)
    return (y * g.astype(jnp.float32)).astype(x.dtype)


def rope(x, pos):
    half = ROT_DIM // 2
    inv = jnp.power(ROPE_THETA, -jnp.arange(half, dtype=jnp.float32) * 2.0 / ROT_DIM)
    ang = pos.astype(jnp.float32)[:, None] * inv[None, :]
    cos = jnp.cos(ang)[:, None, :].astype(x.dtype)
    sin = jnp.sin(ang)[:, None, :].astype(x.dtype)
    x1, x2 = x[..., :half], x[..., half:ROT_DIM]
    return jnp.concatenate([x1 * cos - x2 * sin, x2 * cos + x1 * sin, x[..., ROT_DIM:]], axis=-1)


def sb_attend(q, k, v, q_pos, k_pos, bias):
    z = jnp.einsum('bqhe,bkhe->bhqk', q, k).astype(jnp.float32) * (HEAD_DIM ** -0.5)
    z = z + bias.astype(jnp.float32)[None, :, None, None]
    mask = k_pos[None, :] < q_pos[:, None]
    log_beta = jax.nn.log_sigmoid(z)
    log_rest = jnp.where(mask, jax.nn.log_sigmoid(-z), 0.0)
    tail = lax.cumsum(log_rest, axis=3, reverse=True) - log_rest
    a = jnp.where(mask, jnp.exp(log_beta + tail), 0.0)
    return jnp.einsum('bhqk,bkhe->bqhe', a.astype(v.dtype), v)


def sb_prompt(q, k, v, bias):
    b, s, h, e = q.shape
    nb = s // BLOCK
    qb = q.reshape(b, nb, BLOCK, h, e).swapaxes(0, 1)
    k_pos = jnp.arange(s)

    def one(args):
        q_blk, start = args
        return sb_attend(q_blk, k, v, start + jnp.arange(BLOCK), k_pos, bias)

    o = lax.map(one, (qb, jnp.arange(nb) * BLOCK))
    return o.swapaxes(0, 1).reshape(b, s, h, e)


def local_attend(q, k, v, mask):
    *lead, nq, h, e = q.shape
    hkv = k.shape[-2]
    qg = q.reshape(*lead, nq, hkv, h // hkv, e)
    s = jnp.einsum('...qhge,...che->...hgqc', qg, k).astype(jnp.float32) * (HEAD_DIM ** -0.5)
    s = jnp.where(mask[..., None, None, :, :], s, NEG_INF)
    lse = jax.nn.logsumexp(s, axis=-1)
    p = jnp.exp(s - lse[..., None])
    o = jnp.einsum('...hgqc,...che->...qhge', p.astype(v.dtype), v)
    return o.reshape(*lead, nq, h, e), jnp.moveaxis(lse, -1, -3).reshape(*lead, nq, h)


def split_residues(x, dil, n_blk):
    b, s = x.shape[:2]
    rest = x.shape[2:]
    n_sub = s // dil
    x = x.reshape(b, n_sub, dil, *rest).swapaxes(1, 2)
    x = jnp.pad(x, [(0, 0), (0, 0), (0, n_blk * BLOCK - n_sub)] + [(0, 0)] * len(rest))
    return x.reshape(b, dil, n_blk, BLOCK, *rest)


def merge_residues(x, s):
    b, dil, n_blk = x.shape[:3]
    rest = x.shape[4:]
    x = x.reshape(b, dil, n_blk * BLOCK, *rest)[:, :, : s // dil]
    return x.swapaxes(1, 2).reshape(b, s, *rest)


def band(x):
    xp = jnp.concatenate([jnp.zeros_like(x[:, :, :1]), x], axis=2)
    return jnp.concatenate([xp[:, :, :-1], xp[:, :, 1:]], axis=3)


def dilated_prompt(q, k, v, window, dil):
    s = q.shape[1]
    steps = window // dil
    nb = -(-(s // dil) // BLOCK)
    qb, kb, vb = (split_residues(t, dil, nb) for t in (q, k, v))
    blk = jnp.arange(nb)[:, None, None]
    iq = blk * BLOCK + jnp.arange(BLOCK)[None, :, None]
    ik = (blk - 1) * BLOCK + jnp.arange(2 * BLOCK)[None, None, :]
    rel = iq - ik
    mask = (rel >= 0) & (rel <= steps) & (ik >= 0)
    o, lse = local_attend(qb, band(kb), band(vb), mask)
    return merge_residues(o, s), merge_residues(lse, s)


def dilated_sample(q, buf, kv_new, window, dil):
    comb = jnp.concatenate([buf, kv_new], axis=1)
    lb, t = buf.shape[1], q.shape[1]
    steps = window // dil
    idx = lb + jnp.arange(t)[:, None] - jnp.arange(steps + 1)[None, :] * dil
    valid = idx >= 0
    sel = jnp.take(comb, jnp.maximum(idx, 0), axis=1)
    o, lse = local_attend(q[:, :, None], sel[:, :, :, 0], sel[:, :, :, 1], valid[:, None, :])
    return o[:, :, 0], lse[:, :, 0]


def combine_groups(outs, lses):
    w = jax.nn.softmax(jnp.stack(lses), axis=0).astype(outs[0].dtype)
    return jnp.einsum('gbth,gbthe->bthe', w, jnp.stack(outs))


def shared_kv(x, g_kv, w_kv, k_norm, pos):
    b, t, _ = x.shape
    kv = (rmsnorm(x, g_kv) @ w_kv).reshape(b, t, N_DIL, 2, N_KV_B, HEAD_DIM)
    k = rmsnorm(kv[:, :, :, 0], k_norm[:, None, :])
    k = rope(k.reshape(b, t, N_DIL * N_KV_B, HEAD_DIM), pos).reshape(b, t, N_DIL, N_KV_B, HEAD_DIM)
    return jnp.stack([k, kv[:, :, :, 1]], axis=3)


def b_queries(h, w_q, q_norm, pos):
    b, t, _ = h.shape
    q = rmsnorm((h @ w_q).reshape(b, t, N_DIL, N_HEADS_B, HEAD_DIM), q_norm[:, None, :])
    return rope(q.reshape(b, t, N_DIL * N_HEADS_B, HEAD_DIM), pos).reshape(b, t, N_DIL, N_HEADS_B, HEAD_DIM)


def hmoe(h, w_rg, b_rg, w_re, b_re, w_g, w_u, w_d):
    t = h.shape[0]
    hf = h.astype(jnp.float32)
    g_prob = jax.nn.softmax(hf @ w_rg.astype(jnp.float32) + b_rg.astype(jnp.float32), axis=-1)
    g_idx = jnp.argmax(g_prob, axis=-1)
    g_gate = jnp.max(g_prob, axis=-1)
    e_logits = (hf @ w_re.astype(jnp.float32) + b_re.astype(jnp.float32)).reshape(t, N_GROUPS, EXPERTS_PER_GROUP)
    e_logits = jnp.take_along_axis(e_logits, jnp.broadcast_to(g_idx[:, None, None], (t, 1, EXPERTS_PER_GROUP)), axis=1)[:, 0]
    top_p, top_i = lax.top_k(jax.nn.softmax(e_logits, axis=-1), TOP_K)
    weights = g_gate[:, None] * top_p / jnp.sum(top_p, axis=-1, keepdims=True)
    expert = g_idx[:, None] * EXPERTS_PER_GROUP + top_i
    comb = jnp.einsum('tk,tke->te', weights, jax.nn.one_hot(expert, N_EXPERTS, dtype=jnp.float32)).astype(h.dtype)
    a = jnp.einsum('td,edf->tef', h, w_g)
    u = jnp.einsum('td,edf->tef', h, w_u)
    return jnp.einsum('tef,efd->td', jax.nn.silu(a) * u * comb[:, :, None], w_d)


def setup_inputs(seed: int = 0) -> dict:
    key = jax.random.key(seed)
    ks = jax.random.split(key, 25)
    f32 = jnp.float32

    def nrm(k, shape, scale):
        return jax.random.normal(k, shape, f32) * scale

    n_pages = PAST_LEN // PAGE_SIZE
    n_used = DEC_BATCH * n_pages
    n_pool = n_used + max(1, n_used // 4)
    page_table = jax.random.permutation(ks[0], n_pool)[:n_used].reshape(DEC_BATCH, n_pages).astype(jnp.int32)
    lb = [min(w, PAST_LEN) for w, _ in DILATED]
    sb_bias = -jnp.linspace(SB_BIAS_MIN, SB_BIAS_MAX, N_HEADS_A, dtype=f32)[None, :] + nrm(ks[24], (N_A_LAYERS, N_HEADS_A), 0.1)
    return {
        "x_prompt": nrm(ks[1], (BATCH, SEQ, D_MODEL), 1.0),
        "x_sample": nrm(ks[2], (DEC_BATCH, DEC_SEQ, D_MODEL), 1.0),
        "cache_kv_a": nrm(ks[3], (N_A_LAYERS, n_pool, PAGE_SIZE, 2, N_HEADS_A, HEAD_DIM), 1.0),
        "cache_kv_b1": nrm(ks[4], (DEC_BATCH, lb[0], 2, N_KV_B, HEAD_DIM), 1.0),
        "cache_kv_b2": nrm(ks[5], (DEC_BATCH, lb[1], 2, N_KV_B, HEAD_DIM), 1.0),
        "cache_kv_b3": nrm(ks[6], (DEC_BATCH, lb[2], 2, N_KV_B, HEAD_DIM), 1.0),
        "page_table": page_table,
        "g_attn": 1.0 + nrm(ks[7], (DEPTH, D_MODEL), 0.05),
        "g_ffn": 1.0 + nrm(ks[8], (DEPTH, D_MODEL), 0.05),
        "w_qkv_a": nrm(ks[9], (N_A_LAYERS, D_MODEL, 3 * N_HEADS_A * HEAD_DIM), D_MODEL ** -0.5),
        "w_o_a": nrm(ks[10], (N_A_LAYERS, N_HEADS_A * HEAD_DIM, D_MODEL), (N_HEADS_A * HEAD_DIM) ** -0.5),
        "sb_bias": sb_bias,
        "g_kv": 1.0 + nrm(ks[11], (D_MODEL,), 0.05),
        "w_kv_b": nrm(ks[12], (D_MODEL, N_DIL * 2 * N_KV_B * HEAD_DIM), D_MODEL ** -0.5),
        "k_norm_b": 1.0 + nrm(ks[13], (N_DIL, HEAD_DIM), 0.05),
        "w_q_b": nrm(ks[14], (N_B_LAYERS, D_MODEL, N_DIL * N_HEADS_B * HEAD_DIM), D_MODEL ** -0.5),
        "q_norm_b": 1.0 + nrm(ks[15], (N_B_LAYERS, N_DIL, HEAD_DIM), 0.05),
        "w_o_b": nrm(ks[16], (N_B_LAYERS, N_HEADS_B * HEAD_DIM, D_MODEL), (N_HEADS_B * HEAD_DIM) ** -0.5),
        "w_router_group": nrm(ks[17], (DEPTH, D_MODEL, N_GROUPS), D_MODEL ** -0.5),
        "b_router_group": nrm(ks[18], (DEPTH, N_GROUPS), 0.01),
        "w_router_expert": nrm(ks[19], (DEPTH, D_MODEL, N_EXPERTS), D_MODEL ** -0.5),
        "b_router_expert": nrm(ks[20], (DEPTH, N_EXPERTS), 0.01),
        "w_gate": nrm(ks[21], (DEPTH, N_EXPERTS, D_MODEL, D_EXPERT), D_MODEL ** -0.5),
        "w_up": nrm(ks[22], (DEPTH, N_EXPERTS, D_MODEL, D_EXPERT), D_MODEL ** -0.5),
        "w_down": nrm(ks[23], (DEPTH, N_EXPERTS, D_EXPERT, D_MODEL), D_EXPERT ** -0.5),
    }


def reference(x_prompt, x_sample, cache_kv_a, cache_kv_b1, cache_kv_b2, cache_kv_b3, page_table,
              g_attn, g_ffn, w_qkv_a, w_o_a, sb_bias, g_kv, w_kv_b, k_norm_b, w_q_b, q_norm_b, w_o_b,
              w_router_group, b_router_group, w_router_expert, b_router_expert, w_gate, w_up, w_down):
    b, s, d = x_prompt.shape
    db, t, _ = x_sample.shape
    past_len = page_table.shape[1] * cache_kv_a.shape[2]
    pos_p = jnp.arange(s)
    pos_s = past_len + jnp.arange(t)
    buffers = (cache_kv_b1, cache_kv_b2, cache_kv_b3)
    xp, xs = x_prompt, x_sample
    new_a_p, new_a_s = [], []
    for layer in range(DEPTH):
        if layer < N_A_LAYERS:
            i = layer
            qkv = (rmsnorm(xp, g_attn[layer]) @ w_qkv_a[i]).reshape(b, s, 3, N_HEADS_A, HEAD_DIM)
            o_p = sb_prompt(qkv[:, :, 0], qkv[:, :, 1], qkv[:, :, 2], sb_bias[i])
            xp = xp + o_p.reshape(b, s, -1) @ w_o_a[i]
            new_a_p.append(qkv[:, :, 1:])
            qkv_s = (rmsnorm(xs, g_attn[layer]) @ w_qkv_a[i]).reshape(db, t, 3, N_HEADS_A, HEAD_DIM)
            past = cache_kv_a[i][page_table].reshape(db, past_len, 2, N_HEADS_A, HEAD_DIM)
            k_all = jnp.concatenate([past[:, :, 0], qkv_s[:, :, 1]], axis=1)
            v_all = jnp.concatenate([past[:, :, 1], qkv_s[:, :, 2]], axis=1)
            o_s = sb_attend(qkv_s[:, :, 0], k_all, v_all, pos_s, jnp.arange(past_len + t), sb_bias[i])
            xs = xs + o_s.reshape(db, t, -1) @ w_o_a[i]
            new_a_s.append(qkv_s[:, :, 1:])
        else:
            j = layer - N_A_LAYERS
            if j == 0:
                kv_p = shared_kv(xp, g_kv, w_kv_b, k_norm_b, pos_p)
                kv_s = shared_kv(xs, g_kv, w_kv_b, k_norm_b, pos_s)
            q_p = b_queries(rmsnorm(xp, g_attn[layer]), w_q_b[j], q_norm_b[j], pos_p)
            q_s = b_queries(rmsnorm(xs, g_attn[layer]), w_q_b[j], q_norm_b[j], pos_s)
            outs_p, lses_p, outs_s, lses_s = [], [], [], []
            for g, (window, dil) in enumerate(DILATED):
                o, l = dilated_prompt(q_p[:, :, g], kv_p[:, :, g, 0], kv_p[:, :, g, 1], window, dil)
                outs_p.append(o)
                lses_p.append(l)
                o, l = dilated_sample(q_s[:, :, g], buffers[g], kv_s[:, :, g], window, dil)
                outs_s.append(o)
                lses_s.append(l)
            xp = xp + combine_groups(outs_p, lses_p).reshape(b, s, -1) @ w_o_b[j]
            xs = xs + combine_groups(outs_s, lses_s).reshape(db, t, -1) @ w_o_b[j]
        moe = (w_router_group[layer], b_router_group[layer], w_router_expert[layer], b_router_expert[layer],
               w_gate[layer], w_up[layer], w_down[layer])
        xp = xp + hmoe(rmsnorm(xp, g_ffn[layer]).reshape(b * s, d), *moe).reshape(b, s, d)
        xs = xs + hmoe(rmsnorm(xs, g_ffn[layer]).reshape(db * t, d), *moe).reshape(db, t, d)
    new_kv_a_prompt = jnp.stack(new_a_p)
    new_kv_a_sample = jnp.stack(new_a_s)
    new_kv_b1_prompt = kv_p[:, s - min(DILATED[0][0], s):, 0]
    new_kv_b2_prompt = kv_p[:, s - min(DILATED[1][0], s):, 1]
    new_kv_b3_prompt = kv_p[:, s - min(DILATED[2][0], s):, 2]
    new_kv_b1_sample = kv_s[:, :, 0]
    new_kv_b2_sample = kv_s[:, :, 1]
    new_kv_b3_sample = kv_s[:, :, 2]
    return (xp, xs, new_kv_a_prompt, new_kv_a_sample, new_kv_b1_prompt, new_kv_b1_sample,
            new_kv_b2_prompt, new_kv_b2_sample, new_kv_b3_prompt, new_kv_b3_sample)
```

```python
import functools

import numpy as np
import jax
import jax.numpy as jnp
from jax import lax
from jax.experimental import pallas as pl
from jax.experimental.pallas import tpu as pltpu

F32 = jnp.float32
BF16 = jnp.bfloat16
I32 = jnp.int32

HEAD_DIM = 64
ROT_DIM = HEAD_DIM // 4
ROPE_THETA = 500000.0
DILATED = ((128, 1), (512, 4), (2048, 16))
N_DIL = len(DILATED)
BLOCK = 128
N_GROUPS = 4
EXPERTS_PER_GROUP = 8
N_EXPERTS = N_GROUPS * EXPERTS_PER_GROUP
RMS_EPS = 1e-6
NEG_INF = -1e30

LANES = 128
SUBLANES = 8
VMEM_LIMIT = 48 * 1024 * 1024
GROUP_LANE0 = N_EXPERTS


def _dot(a, b):
    return jnp.dot(a, b, preferred_element_type=F32)


def _dot_nt(a, b):
    return lax.dot_general(a, b, (((1,), (1,)), ((), ())), preferred_element_type=F32)


def _split2(x):
    hi = x.astype(BF16)
    lo = (x - hi.astype(F32)).astype(BF16)
    return hi, lo


def _split3(x):
    hi = x.astype(BF16)
    r = x - hi.astype(F32)
    mid = r.astype(BF16)
    lo = (r - mid.astype(F32)).astype(BF16)
    return hi, mid, lo


def _rms_scale(x):
    return x * lax.rsqrt(jnp.mean(x * x, axis=-1, keepdims=True) + RMS_EPS)


def _params(sem):
    return pltpu.CompilerParams(dimension_semantics=sem, vmem_limit_bytes=VMEM_LIMIT)


def _proj_a_kernel(x_ref, g_ref, w_ref, o_ref, *, tn):
    xn = (_rms_scale(x_ref[...]) * g_ref[...]).astype(BF16)
    for c in range(o_ref.shape[-1] // tn):
        o_ref[:, c * tn:(c + 1) * tn] = _dot(xn, w_ref[:, c * tn:(c + 1) * tn])


def _proj_a(x, g, w_bf, tm):
    t, d = x.shape
    n = w_bf.shape[1]
    return pl.pallas_call(
        functools.partial(_proj_a_kernel, tn=512),
        grid=(t // tm,),
        in_specs=[pl.BlockSpec((tm, d), lambda i: (i, 0)),
                  pl.BlockSpec((1, d), lambda i: (0, 0)),
                  pl.BlockSpec((d, n), lambda i: (0, 0))],
        out_specs=pl.BlockSpec((tm, n), lambda i: (i, 0)),
        out_shape=jax.ShapeDtypeStruct((t, n), F32),
        compiler_params=_params(("parallel",)),
        name="proj_a",
    )(x, g.reshape(1, d), w_bf)


def _headnorm_rope(y, gain, bd, c, s1, s2):
    hi, lo = _split2(y * y)
    ms = (_dot(hi, bd) + _dot(lo, bd)) * (1.0 / HEAD_DIM)
    yn = y * lax.rsqrt(ms + RMS_EPS) * gain
    half = ROT_DIM // 2
    return yn * c + pltpu.roll(yn, LANES - half, 1) * s1 + pltpu.roll(yn, half, 1) * s2


def _proj_b_kernel(x_ref, gkv_ref, wkv_ref, gq_ref, wq_ref, gainkv_ref, gainq_ref, bd_ref,
                   c_ref, s1_ref, s2_ref, kv_ref, q_ref, *, tn):
    xh = _rms_scale(x_ref[...])
    bd = bd_ref[...]
    c, s1, s2 = c_ref[...], s1_ref[...], s2_ref[...]
    kv_cols = 2 * (kv_ref.shape[-1] // (2 * N_DIL))

    def run(g_ref, w_ref, gain_ref, o_ref, is_key):
        xn = (xh * g_ref[...]).astype(BF16)
        for cc in range(o_ref.shape[-1] // tn):
            y = _dot(xn, w_ref[:, cc * tn:(cc + 1) * tn])
            for s in range(tn // LANES):
                col = cc * tn + s * LANES
                ys = y[:, s * LANES:(s + 1) * LANES]
                if is_key(col):
                    ys = _headnorm_rope(ys, gain_ref[:, col:col + LANES], bd, c, s1, s2)
                o_ref[:, col:col + LANES] = ys

    run(gkv_ref, wkv_ref, gainkv_ref, kv_ref, lambda col: (col % kv_cols) < kv_cols // 2)
    run(gq_ref, wq_ref, gainq_ref, q_ref, lambda col: True)


def _rope_tables(pos):
    half = ROT_DIM // 2
    inv = jnp.power(ROPE_THETA, -jnp.arange(half, dtype=F32) * 2.0 / ROT_DIM)
    ang = pos.astype(F32)[:, None] * inv[None, :]
    cos, sin = jnp.cos(ang), jnp.sin(ang)
    n = pos.shape[0]
    ones = jnp.ones((n, HEAD_DIM - ROT_DIM), F32)
    zeros = jnp.zeros((n, HEAD_DIM - ROT_DIM), F32)
    zh = jnp.zeros((n, half), F32)
    c = jnp.concatenate([cos, cos, ones], axis=1)
    s1 = jnp.concatenate([-sin, zh, zeros], axis=1)
    s2 = jnp.concatenate([zh, sin, zeros], axis=1)
    rep = LANES // HEAD_DIM
    return tuple(jnp.tile(a, (1, rep)) for a in (c, s1, s2))


def _proj_b(x, g_kv, wkv_bf, g_q, wq_bf, k_norm, q_norm, tables, tm, n_heads_q, n_kv):
    t, d = x.shape
    nkv, nq = wkv_bf.shape[1], wq_bf.shape[1]
    gain_kv = jnp.broadcast_to(k_norm[:, None, None, :], (N_DIL, 2, n_kv, HEAD_DIM)).reshape(1, nkv)
    gain_q = jnp.broadcast_to(q_norm[:, None, :], (N_DIL, n_heads_q, HEAD_DIM)).reshape(1, nq)
    lane = np.arange(LANES)
    bd = jnp.asarray((lane[:, None] // HEAD_DIM == lane[None, :] // HEAD_DIM), BF16)
    n_tab = tables[0].shape[0] // tm
    tab_spec = pl.BlockSpec((tm, LANES), lambda i: (i % n_tab, 0))
    full = lambda shape: pl.BlockSpec(shape, lambda i: (0, 0))
    return pl.pallas_call(
        functools.partial(_proj_b_kernel, tn=512),
        grid=(t // tm,),
        in_specs=[pl.BlockSpec((tm, d), lambda i: (i, 0)),
                  full((1, d)), full((d, nkv)), full((1, d)), full((d, nq)),
                  full((1, nkv)), full((1, nq)), full((LANES, LANES)),
                  tab_spec, tab_spec, tab_spec],
        out_specs=[pl.BlockSpec((tm, nkv), lambda i: (i, 0)),
                   pl.BlockSpec((tm, nq), lambda i: (i, 0))],
        out_shape=[jax.ShapeDtypeStruct((t, nkv), F32), jax.ShapeDtypeStruct((t, nq), F32)],
        compiler_params=_params(("parallel",)),
        name="proj_b",
    )(x, g_kv.reshape(1, d), wkv_bf, g_q.reshape(1, d), wq_bf, gain_kv, gain_q, bd, *tables)


def _sb_prompt_kernel(bias_ref, pm_ref, q_ref, k_ref, v_ref, o_ref, kbf, vtb, acc_a, acc_b, *, blk):
    hp = pl.program_id(1)
    qi = pl.program_id(2)
    seg = blk // SUBLANES
    n_blocks = kbf.shape[0]

    @pl.when(qi == 0)
    def _():
        pm = pm_ref[...]

        def prep(j, carry):
            rows = pl.ds(pl.multiple_of(j * blk, blk), blk)
            kbf[j] = _dot(pm, k_ref[rows, :].astype(BF16)).astype(BF16)
            vtb[j] = _dot(pm, v_ref[rows, :].astype(BF16)).T.astype(BF16)
            return carry

        lax.fori_loop(0, n_blocks, prep, 0)

    q = q_ref[...] * (HEAD_DIM ** -0.5)
    lane = lax.broadcasted_iota(I32, q.shape, 1)
    q_a = jnp.where(lane < HEAD_DIM, q, 0.0).astype(BF16)
    q_b = jnp.where(lane >= HEAD_DIM, q, 0.0).astype(BF16)
    bias_a = bias_ref[2 * hp]
    bias_b = bias_ref[2 * hp + 1]

    rho = lax.broadcasted_iota(I32, (blk, blk), 0)
    key_idx = (rho & (SUBLANES - 1)) * seg + (rho >> 3)
    diag_mask = key_idx < lax.broadcasted_iota(I32, (blk, blk), 1)
    sub = lax.broadcasted_iota(I32, (SUBLANES, blk), 0)

    def head_block(kj, vtj, qh, bias, carry, masked):
        z = _dot_nt(kj, qh) + bias
        e = jnp.exp(-jnp.abs(z))
        r = 1.0 / (1.0 + e)
        s = e * r
        pos = z >= 0.0
        beta = jnp.where(pos, r, s)
        om = jnp.where(pos, s, r)
        if masked:
            beta = jnp.where(diag_mask, beta, 0.0)
            om = jnp.where(diag_mask, om, 1.0)
        tot = om[(seg - 1) * SUBLANES:seg * SUBLANES, :]
        for v in range(seg - 2, -1, -1):
            tot = tot * om[v * SUBLANES:(v + 1) * SUBLANES, :]
        inc = tot
        for d in (1, 2, 4):
            inc = inc * jnp.where(sub + d < SUBLANES, pltpu.roll(inc, SUBLANES - d, 0), 1.0)
        exc = jnp.where(sub + 1 < SUBLANES, pltpu.roll(inc, SUBLANES - 1, 0), 1.0)
        run = exc * carry
        new_carry = inc[0:1, :] * carry
        rows = [None] * seg
        for v in range(seg - 1, -1, -1):
            sl = slice(v * SUBLANES, (v + 1) * SUBLANES)
            rows[v] = beta[sl, :] * run
            run = run * om[sl, :]
        a = jnp.concatenate(rows, axis=0).astype(BF16)
        return _dot(vtj, a), new_carry

    ones = jnp.ones((SUBLANES, blk), F32)
    kd, vd = kbf[qi], vtb[qi]
    pv_a, c_a = head_block(kd, vd, q_a, bias_a, ones, True)
    pv_b, c_b = head_block(kd, vd, q_b, bias_b, ones, True)
    acc_a[...] = pv_a
    acc_b[...] = pv_b

    def body(t, carry):
        c_a, c_b = carry
        j = qi - 1 - t
        kj, vj = kbf[j], vtb[j]
        pv_a, c_a = head_block(kj, vj, q_a, bias_a, c_a, False)
        pv_b, c_b = head_block(kj, vj, q_b, bias_b, c_b, False)
        acc_a[...] += pv_a
        acc_b[...] += pv_b
        return c_a, c_b

    lax.fori_loop(0, qi, body, (c_a, c_b))
    row = lax.broadcasted_iota(I32, acc_a.shape, 0)
    o_ref[...] = jnp.where(row < HEAD_DIM, acc_a[...], acc_b[...]).T


def _sb_prompt(qkv, bias, n_heads, blk=256):
    b, s, _ = qkv.shape
    n_hp = n_heads * HEAD_DIM // LANES
    n_blocks = s // blk
    seg = blk // SUBLANES
    rho = np.arange(blk)
    pm = jnp.asarray(((rho % SUBLANES) * seg + rho // SUBLANES)[:, None] == np.arange(blk)[None, :], BF16)
    return pl.pallas_call(
        functools.partial(_sb_prompt_kernel, blk=blk),
        grid=(b, n_hp, n_blocks),
        in_specs=[pl.BlockSpec(memory_space=pltpu.SMEM),
                  pl.BlockSpec((blk, blk), lambda bi, hp, qi: (0, 0)),
                  pl.BlockSpec((None, blk, LANES), lambda bi, hp, qi: (bi, qi, hp)),
                  pl.BlockSpec((None, s, LANES), lambda bi, hp, qi: (bi, 0, n_hp + hp)),
                  pl.BlockSpec((None, s, LANES), lambda bi, hp, qi: (bi, 0, 2 * n_hp + hp))],
        out_specs=pl.BlockSpec((None, blk, LANES), lambda bi, hp, qi: (bi, qi, hp)),
        out_shape=jax.ShapeDtypeStruct((b, s, n_heads * HEAD_DIM), F32),
        scratch_shapes=[pltpu.VMEM((n_blocks, blk, LANES), BF16),
                        pltpu.VMEM((n_blocks, LANES, blk), BF16),
                        pltpu.VMEM((LANES, blk), F32),
                        pltpu.VMEM((LANES, blk), F32)],
        compiler_params=_params(("parallel", "parallel", "arbitrary")),
        name="sb_prompt",
    )(bias, pm, qkv, qkv, qkv)


def _softplus(x):
    return jnp.maximum(x, 0.0) + jnp.log1p(jnp.exp(-jnp.abs(x)))


def _sb_sample_kernel(pt_ref, q_ref, kv_ref, bias_ref, hmask_ref, ut_ref, o_ref, carry, acc):
    p = pl.program_id(1)
    hd = q_ref.shape[-1]

    @pl.when(p == 0)
    def _():
        carry[...] = jnp.zeros_like(carry)
        acc[...] = jnp.zeros_like(acc)

    hmask = hmask_ref[...]
    q_rows = (hmask * (q_ref[...] * (HEAD_DIM ** -0.5))).astype(BF16)
    k = kv_ref[:, :hd].astype(BF16)
    v = kv_ref[:, hd:].astype(BF16)
    z = _dot_nt(q_rows, k) + bias_ref[...]
    log_rest = -_softplus(z)
    log_beta = log_rest + z
    ut = ut_ref[...]
    hi, mid, lo = _split3(log_rest)
    tail = _dot(hi, ut) + _dot(mid, ut) + _dot(lo, ut) + carry[...]
    a = jnp.exp(log_beta + tail)
    acc[...] += _dot(a.astype(BF16), v)
    carry[...] += jnp.sum(log_rest, axis=-1, keepdims=True)

    @pl.when(p == pl.num_programs(1) - 1)
    def _():
        o_ref[...] = jnp.sum(acc[...] * hmask, axis=0, keepdims=True)


def _sb_sample(q3, cache, page_table, bias, n_heads):
    db, n_pages = page_table.shape
    page = cache.shape[1]
    hd = n_heads * HEAD_DIM
    hmask = jnp.asarray(np.arange(hd)[None, :] // HEAD_DIM == np.arange(n_heads)[:, None], F32)
    ut = jnp.asarray(np.arange(page)[:, None] > np.arange(page)[None, :], BF16)
    bias2 = jnp.broadcast_to(bias[:, None], (n_heads, page))
    grid_spec = pltpu.PrefetchScalarGridSpec(
        num_scalar_prefetch=1,
        grid=(db, n_pages),
        in_specs=[pl.BlockSpec((None, 1, hd), lambda b, p, pt: (b, 0, 0)),
                  pl.BlockSpec((None, page, 2 * hd), lambda b, p, pt: (pt[b, n_pages - 1 - p], 0, 0)),
                  pl.BlockSpec((n_heads, page), lambda b, p, pt: (0, 0)),
                  pl.BlockSpec((n_heads, hd), lambda b, p, pt: (0, 0)),
                  pl.BlockSpec((page, page), lambda b, p, pt: (0, 0))],
        out_specs=pl.BlockSpec((None, 1, hd), lambda b, p, pt: (b, 0, 0)),
        scratch_shapes=[pltpu.VMEM((n_heads, page), F32), pltpu.VMEM((n_heads, hd), F32)])
    return pl.pallas_call(
        _sb_sample_kernel,
        grid_spec=grid_spec,
        out_shape=jax.ShapeDtypeStruct((db, 1, hd), F32),
        compiler_params=_params(("parallel", "arbitrary")),
        name="sb_sample",
    )(page_table, q3, cache, bias2, hmask, ut)


def _dil_prompt_kernel(q_ref, kp_ref, kc_ref, vp_ref, vc_ref, rep_ref, o_ref, lse_ref, *, steps, n_kv):
    i = pl.program_id(2)
    blk = q_ref.shape[0]
    width = kc_ref.shape[-1]
    per_kv = q_ref.shape[-1] // width
    k2 = jnp.concatenate([kp_ref[...], kc_ref[...]], axis=0).astype(BF16)
    v2 = jnp.concatenate([vp_ref[...], vc_ref[...]], axis=0).astype(BF16)
    row = lax.broadcasted_iota(I32, (blk, 2 * blk), 0)
    col = lax.broadcasted_iota(I32, (blk, 2 * blk), 1)
    rel = blk + row - col
    first_col = jnp.where(i > 0, 0, blk)
    mask = (rel >= 0) & (rel <= steps) & (col >= first_col)
    lane = lax.broadcasted_iota(I32, (blk, width), 1)
    lane_l = lax.broadcasted_iota(I32, (blk, LANES), 1)
    lse_tile = jnp.zeros((blk, LANES), F32)
    scale = HEAD_DIM ** -0.5
    for c in range(n_kv):
        rep = rep_ref[c]
        k_rep = _dot(k2, rep).astype(BF16)
        v_rep = _dot(v2, rep).astype(BF16)
        qc = q_ref[:, c * width:(c + 1) * width] * scale
        o_c = jnp.zeros((blk, width), F32)
        for j in range(per_kv):
            sel = (lane >= j * HEAD_DIM) & (lane < (j + 1) * HEAD_DIM)
            qj = jnp.where(sel, qc, 0.0).astype(BF16)
            s = jnp.where(mask, _dot_nt(qj, k_rep), NEG_INF)
            m = jnp.max(s, axis=-1, keepdims=True)
            pexp = jnp.exp(s - m)
            den = jnp.sum(pexp, axis=-1, keepdims=True)
            lse = m + jnp.log(den)
            p = jnp.exp(s - lse)
            o_c = o_c + jnp.where(sel, _dot(p.astype(BF16), v_rep), 0.0)
            lse_tile = lse_tile + jnp.where(lane_l == c * per_kv + j, lse, 0.0)
        o_ref[:, c * width:(c + 1) * width] = o_c
    lse_ref[...] = lse_tile


def _rep_matrices(n_kv):
    width = n_kv * HEAD_DIM
    src = np.arange(width)[:, None]
    dst = np.arange(width)[None, :]
    return jnp.asarray(np.stack([(src == c * HEAD_DIM + dst % HEAD_DIM) for c in range(n_kv)]), BF16)


def _dil_prompt(q, kv, g, n_heads, n_kv):
    b, s, nq = q.shape
    nkv = kv.shape[-1]
    window, dil = DILATED[g]
    steps = window // dil
    hd = n_heads * HEAD_DIM
    width = n_kv * HEAD_DIM
    n_sub = s // dil
    nb = n_sub // BLOCK
    qv = q.reshape(b, n_sub, dil * nq)
    kvv = kv.reshape(b, n_sub, dil * nkv)
    q_blocks = nq // hd
    kv_blocks = nkv // width
    kcol = lambda r: r * kv_blocks + g * 2
    out = pl.pallas_call(
        functools.partial(_dil_prompt_kernel, steps=steps, n_kv=n_kv),
        grid=(b, dil, nb),
        in_specs=[pl.BlockSpec((None, BLOCK, hd), lambda bi, r, i: (bi, i, r * q_blocks + g)),
                  pl.BlockSpec((None, BLOCK, width), lambda bi, r, i: (bi, jnp.maximum(i - 1, 0), kcol(r))),
                  pl.BlockSpec((None, BLOCK, width), lambda bi, r, i: (bi, i, kcol(r))),
                  pl.BlockSpec((None, BLOCK, width), lambda bi, r, i: (bi, jnp.maximum(i - 1, 0), kcol(r) + 1)),
                  pl.BlockSpec((None, BLOCK, width), lambda bi, r, i: (bi, i, kcol(r) + 1)),
                  pl.BlockSpec((n_kv, width, width), lambda bi, r, i: (0, 0, 0))],
        out_specs=[pl.BlockSpec((None, BLOCK, hd), lambda bi, r, i: (bi, i, r)),
                   pl.BlockSpec((None, BLOCK, LANES), lambda bi, r, i: (bi, i, r))],
        out_shape=[jax.ShapeDtypeStruct((b, n_sub, dil * hd), F32),
                   jax.ShapeDtypeStruct((b, n_sub, dil * LANES), F32)],
        compiler_params=_params(("parallel", "parallel", "parallel")),
        name=f"dil_prompt_{g}",
    )(qv, kvv, kvv, kvv, kvv, _rep_matrices(n_kv))
    return out[0].reshape(b * s, hd), out[1].reshape(b * s, LANES)


def _dil_sample_kernel(q_ref, buf_ref, kvn_ref, fold_ref, unfold_ref, hmask_ref, o_ref, lse_ref, *, nseq, n_valid):
    width = kvn_ref.shape[-1] // 2
    n_heads, hd = hmask_ref.shape
    n_keys = buf_ref.shape[1]
    hmask = hmask_ref[...]
    fold = fold_ref[...]
    unfold = unfold_ref[...]
    scale = HEAD_DIM ** -0.5
    key_ok = lax.broadcasted_iota(I32, (n_heads, n_keys), 1) >= n_keys - n_valid
    eye = (lax.broadcasted_iota(I32, (n_heads, LANES), 0) == lax.broadcasted_iota(I32, (n_heads, LANES), 1))
    for n in range(nseq):
        q_rows = (hmask * (q_ref[n] * scale)).astype(BF16)
        qm = _dot(q_rows, fold)
        k = buf_ref[n, :, :width].astype(BF16)
        v = buf_ref[n, :, width:].astype(BF16)
        k_new = kvn_ref[n, :, :width]
        v_new = kvn_ref[n, :, width:]
        s = jnp.where(key_ok, _dot_nt(qm.astype(BF16), k), NEG_INF)
        s_new = jnp.sum(qm * k_new, axis=-1, keepdims=True)
        m = jnp.maximum(jnp.max(s, axis=-1, keepdims=True), s_new)
        den = jnp.sum(jnp.exp(s - m), axis=-1, keepdims=True) + jnp.exp(s_new - m)
        lse = m + jnp.log(den)
        p = jnp.exp(s - lse)
        p_new = jnp.exp(s_new - lse)
        o = _dot(p.astype(BF16), v) + p_new * v_new
        o_full = _dot(o.astype(BF16), unfold) * hmask
        o_ref[n] = jnp.sum(o_full, axis=0, keepdims=True)
        lse_ref[n] = jnp.sum(jnp.where(eye, lse, 0.0), axis=0, keepdims=True)


def _dil_sample(q3, buf, kv3, g, n_heads, n_kv, nseq=8):
    db = q3.shape[0]
    window, dil = DILATED[g]
    steps = window // dil
    hd = n_heads * HEAD_DIM
    width = n_kv * HEAD_DIM
    lb = buf.shape[1]
    n_keys = -(-lb // dil)
    first = lb - n_keys * dil
    n_valid = min(steps, lb // dil)
    if first < 0:
        buf = jnp.pad(buf, ((0, 0), (-first, 0), (0, 0)))
    bufv = buf.reshape(db, n_keys, dil * 2 * width)
    per_kv = n_heads // n_kv
    src = np.arange(hd)[:, None]
    dst = np.arange(width)[None, :]
    fold_np = (src // (HEAD_DIM * per_kv) == dst // HEAD_DIM) & (src % HEAD_DIM == dst % HEAD_DIM)
    fold = jnp.asarray(fold_np, BF16)
    unfold = jnp.asarray(fold_np.T, BF16)
    hmask = jnp.asarray(np.arange(hd)[None, :] // HEAD_DIM == np.arange(n_heads)[:, None], F32)
    full2 = lambda shape: pl.BlockSpec(shape, lambda i: (0, 0))
    o, lse = pl.pallas_call(
        functools.partial(_dil_sample_kernel, nseq=nseq, n_valid=n_valid),
        grid=(db // nseq,),
        in_specs=[pl.BlockSpec((nseq, 1, hd), lambda i: (i, 0, g)),
                  pl.BlockSpec((nseq, n_keys, 2 * width), lambda i: (i, 0, 0)),
                  pl.BlockSpec((nseq, 1, 2 * width), lambda i: (i, 0, g)),
                  full2((hd, width)), full2((width, hd)), full2((n_heads, hd))],
        out_specs=[pl.BlockSpec((nseq, 1, hd), lambda i: (i, 0, 0)),
                   pl.BlockSpec((nseq, 1, LANES), lambda i: (i, 0, 0))],
        out_shape=[jax.ShapeDtypeStruct((db, 1, hd), F32), jax.ShapeDtypeStruct((db, 1, LANES), F32)],
        compiler_params=_params(("parallel",)),
        name=f"dil_sample_{g}",
    )(q3, bufv, kv3, fold, unfold, hmask)
    return o.reshape(db, hd), lse.reshape(db, LANES)


def _route(h, wr_ref, br_ref, ltri_ref, carry_ref):
    tm = h.shape[0]
    h3 = _split3(h)
    w3 = _split3(wr_ref[...])
    logits = br_ref[...]
    for a, b in ((0, 0), (0, 1), (1, 0), (1, 1), (0, 2), (2, 0)):
        logits = logits + _dot(h3[a], w3[b])
    lane = lax.broadcasted_iota(I32, (tm, LANES), 1)
    lane_f = lane.astype(F32)
    big = float(LANES)
    is_group = (lane >= GROUP_LANE0) & (lane < GROUP_LANE0 + N_GROUPS)
    gl = jnp.where(is_group, logits, NEG_INF)
    gmax = jnp.max(gl, axis=-1, keepdims=True)
    gate = 1.0 / jnp.sum(jnp.exp(gl - gmax), axis=-1, keepdims=True)
    gidx = jnp.min(jnp.where(gl == gmax, lane_f, big), axis=-1, keepdims=True) - GROUP_LANE0
    lo = gidx * EXPERTS_PER_GROUP
    in_group = (lane_f >= lo) & (lane_f < lo + EXPERTS_PER_GROUP)
    el = jnp.where(in_group, logits, NEG_INF)
    m1 = jnp.max(el, axis=-1, keepdims=True)
    i1 = jnp.min(jnp.where(el == m1, lane_f, big), axis=-1, keepdims=True)
    el2 = jnp.where(lane_f == i1, NEG_INF, el)
    m2 = jnp.max(el2, axis=-1, keepdims=True)
    i2 = jnp.min(jnp.where(el2 == m2, lane_f, big), axis=-1, keepdims=True)
    t = jnp.exp(m2 - m1)
    w1 = gate / (1.0 + t)
    w2 = gate * t / (1.0 + t)
    hit1 = lane_f == i1
    hit2 = lane_f == i2
    onehot = (hit1 | hit2).astype(F32)
    ranks = _dot(ltri_ref[...], onehot.astype(BF16)) + carry_ref[0:1, :]
    r1 = jnp.sum(jnp.where(hit1, ranks, 0.0), axis=-1, keepdims=True)
    r2 = jnp.sum(jnp.where(hit2, ranks, 0.0), axis=-1, keepdims=True)
    carry_ref[...] += jnp.sum(onehot, axis=0, keepdims=True)
    ints = (jnp.where(lane == 0, i1, 0.0) + jnp.where(lane == 1, i2, 0.0)
            + jnp.where(lane == 2, r1, 0.0) + jnp.where(lane == 3, r2, 0.0))
    wts = jnp.where(lane == 0, w1, 0.0) + jnp.where(lane == 1, w2, 0.0)
    return ints, wts


def _oproj_router_kernel(*refs, combine):
    if combine:
        (x_ref, o1_ref, o2_ref, o3_ref, l1_ref, l2_ref, l3_ref, exp_ref, wo_ref, g_ref, wr_ref, br_ref,
         ltri_ref, y_ref, h_ref, rw_ref, ri_ref, cnt_ref, carry_ref) = refs
        l1, l2, l3 = l1_ref[...], l2_ref[...], l3_ref[...]
        m = jnp.maximum(jnp.maximum(l1, l2), l3)
        e1, e2, e3 = jnp.exp(l1 - m), jnp.exp(l2 - m), jnp.exp(l3 - m)
        inv = 1.0 / (e1 + e2 + e3)
        ex = exp_ref[...]
        o = jnp.zeros(o1_ref.shape, F32)
        for e, o_ref in ((e1, o1_ref), (e2, o2_ref), (e3, o3_ref)):
            hi, lo = _split2(e * inv)
            o = o + (_dot(hi, ex) + _dot(lo, ex)) * o_ref[...]
    else:
        (x_ref, o_ref, wo_ref, g_ref, wr_ref, br_ref, ltri_ref,
         y_ref, h_ref, rw_ref, ri_ref, cnt_ref, carry_ref) = refs
        o = o_ref[...]

    @pl.when(pl.program_id(0) == 0)
    def _():
        carry_ref[...] = jnp.zeros_like(carry_ref)

    y = x_ref[...] + _dot(o.astype(BF16), wo_ref[...])
    y_ref[...] = y
    h = _rms_scale(y) * g_ref[...]
    h_ref[...] = h
    ints, wts = _route(h, wr_ref, br_ref, ltri_ref, carry_ref)
    rw_ref[...] = wts
    ri_ref[...] = ints.T[0:SUBLANES, :].astype(I32)
    cnt_ref[...] = carry_ref[...]


def _oproj_router(x, attn, wo_bf, g_ffn, w_rg, b_rg, w_re, b_re, tm):
    t, d = x.shape
    hd = wo_bf.shape[0]
    nt = t // tm
    wr = jnp.zeros((d, LANES), F32).at[:, :N_EXPERTS].set(w_re).at[:, GROUP_LANE0:GROUP_LANE0 + N_GROUPS].set(w_rg)
    br = jnp.zeros((1, LANES), F32).at[0, :N_EXPERTS].set(b_re).at[0, GROUP_LANE0:GROUP_LANE0 + N_GROUPS].set(b_rg)
    ltri = jnp.asarray(np.arange(tm)[:, None] > np.arange(tm)[None, :], BF16)
    row = lambda n: pl.BlockSpec((tm, n), lambda i: (i, 0))
    full = lambda shape: pl.BlockSpec(shape, lambda i: (0, 0))
    combine = len(attn) > 1
    if combine:
        ex = jnp.asarray(np.arange(LANES)[:, None] == np.arange(hd)[None, :] // HEAD_DIM, BF16)
        attn_specs = [row(hd)] * 3 + [row(LANES)] * 3 + [full((LANES, hd))]
        attn_args = list(attn) + [ex]
    else:
        attn_specs = [row(hd)]
        attn_args = list(attn)
    y, h, rw, ri, cnt = pl.pallas_call(
        functools.partial(_oproj_router_kernel, combine=combine),
        grid=(nt,),
        in_specs=[row(d)] + attn_specs + [full((hd, d)), full((1, d)), full((d, LANES)), full((1, LANES)),
                                          full((tm, tm))],
        out_specs=[row(d), row(d), row(LANES),
                   pl.BlockSpec((None, SUBLANES, tm), lambda i: (i, 0, 0)),
                   full((SUBLANES, LANES))],
        out_shape=[jax.ShapeDtypeStruct((t, d), F32), jax.ShapeDtypeStruct((t, d), F32),
                   jax.ShapeDtypeStruct((t, LANES), F32),
                   jax.ShapeDtypeStruct((nt, SUBLANES, tm), I32),
                   jax.ShapeDtypeStruct((SUBLANES, LANES), F32)],
        scratch_shapes=[pltpu.VMEM((SUBLANES, LANES), F32)],
        compiler_params=_params(("arbitrary",)),
        name="oproj_router_b" if combine else "oproj_router_a",
    )(x, *attn_args, wo_bf, g_ffn.reshape(1, d), wr, br, ltri)
    return y, h, rw, ri, cnt[0, :N_EXPERTS].astype(I32)


def _dispatch_kernel(offs_ref, ri_ref, h_ref, hs_ref, sem):
    tm = ri_ref.shape[-1]
    base = pl.program_id(0) * tm

    def issue(r, carry):
        src = h_ref.at[pl.ds(base + r, 1)]
        s1 = offs_ref[ri_ref[0, r]] + ri_ref[2, r]
        s2 = offs_ref[ri_ref[1, r]] + ri_ref[3, r]
        pltpu.make_async_copy(src, hs_ref.at[pl.ds(s1, 1)], sem).start()
        pltpu.make_async_copy(src, hs_ref.at[pl.ds(s2, 1)], sem).start()
        return carry

    lax.fori_loop(0, tm, issue, 0)

    def drain(r, carry):
        pltpu.make_async_copy(h_ref.at[pl.ds(0, 1)], hs_ref.at[pl.ds(0, 1)], sem).wait()
        pltpu.make_async_copy(h_ref.at[pl.ds(0, 1)], hs_ref.at[pl.ds(0, 1)], sem).wait()
        return carry

    lax.fori_loop(0, tm, drain, 0)


def _dispatch(h, ri, offs, n_rows):
    t, d = h.shape
    nt, _, tm = ri.shape
    grid_spec = pltpu.PrefetchScalarGridSpec(
        num_scalar_prefetch=1,
        grid=(nt,),
        in_specs=[pl.BlockSpec((None, SUBLANES, tm), lambda i, offs: (i, 0, 0), memory_space=pltpu.SMEM),
                  pl.BlockSpec(memory_space=pl.ANY)],
        out_specs=pl.BlockSpec(memory_space=pl.ANY),
        scratch_shapes=[pltpu.SemaphoreType.DMA(())])
    return pl.pallas_call(
        _dispatch_kernel,
        grid_spec=grid_spec,
        out_shape=jax.ShapeDtypeStruct((n_rows, d), F32),
        compiler_params=pltpu.CompilerParams(dimension_semantics=("arbitrary",), has_side_effects=True),
        name="moe_dispatch",
    )(offs, ri, h)


def _experts_kernel(blk_ref, exp_ref, used_ref, x_ref, wg_ref, wu_ref, wd_ref, o_ref):
    i = pl.program_id(0)

    @pl.when(i < used_ref[0])
    def _():
        x = x_ref[...].astype(BF16)
        a = _dot(x, wg_ref[...])
        u = _dot(x, wu_ref[...])
        mid = a * (1.0 / (1.0 + jnp.exp(-a))) * u
        o_ref[...] = _dot(mid.astype(BF16), wd_ref[...])


def _experts(hs, wg_bf, wu_bf, wd_bf, tile_blk, tile_exp, n_used, tm):
    n_rows, d = hs.shape
    f = wg_bf.shape[-1]
    n_tiles = n_rows // tm
    grid_spec = pltpu.PrefetchScalarGridSpec(
        num_scalar_prefetch=3,
        grid=(n_tiles,),
        in_specs=[pl.BlockSpec((tm, d), lambda i, blk, ex, used: (blk[i], 0)),
                  pl.BlockSpec((None, d, f), lambda i, blk, ex, used: (ex[i], 0, 0)),
                  pl.BlockSpec((None, d, f), lambda i, blk, ex, used: (ex[i], 0, 0)),
                  pl.BlockSpec((None, f, d), lambda i, blk, ex, used: (ex[i], 0, 0))],
        out_specs=pl.BlockSpec((tm, d), lambda i, blk, ex, used: (blk[i], 0)))
    return pl.pallas_call(
        _experts_kernel,
        grid_spec=grid_spec,
        out_shape=jax.ShapeDtypeStruct((n_rows, d), F32),
        compiler_params=_params(("arbitrary",)),
        name="moe_experts",
    )(tile_blk, tile_exp, n_used, hs, wg_bf, wu_bf, wd_bf)


def _combine_kernel(offs_ref, ri_ref, y_ref, rw_ref, ys_ref, o_ref, g1, g2, sem):
    tm = y_ref.shape[0]

    def issue(r, carry):
        s1 = offs_ref[ri_ref[0, r]] + ri_ref[2, r]
        s2 = offs_ref[ri_ref[1, r]] + ri_ref[3, r]
        pltpu.make_async_copy(ys_ref.at[pl.ds(s1, 1)], g1.at[pl.ds(r, 1)], sem).start()
        pltpu.make_async_copy(ys_ref.at[pl.ds(s2, 1)], g2.at[pl.ds(r, 1)], sem).start()
        return carry

    lax.fori_loop(0, tm, issue, 0)

    def drain(r, carry):
        pltpu.make_async_copy(ys_ref.at[pl.ds(0, 1)], g1.at[pl.ds(0, 1)], sem).wait()
        pltpu.make_async_copy(ys_ref.at[pl.ds(0, 1)], g2.at[pl.ds(0, 1)], sem).wait()
        return carry

    lax.fori_loop(0, tm, drain, 0)
    rw = rw_ref[...]
    o_ref[...] = y_ref[...] + rw[:, 0:1] * g1[...] + rw[:, 1:2] * g2[...]


def _combine(y, rw, ri, offs, ys):
    t, d = y.shape
    nt, _, tm = ri.shape
    grid_spec = pltpu.PrefetchScalarGridSpec(
        num_scalar_prefetch=1,
        grid=(nt,),
        in_specs=[pl.BlockSpec((None, SUBLANES, tm), lambda i, offs: (i, 0, 0), memory_space=pltpu.SMEM),
                  pl.BlockSpec((tm, d), lambda i, offs: (i, 0)),
                  pl.BlockSpec((tm, LANES), lambda i, offs: (i, 0)),
                  pl.BlockSpec(memory_space=pl.ANY)],
        out_specs=pl.BlockSpec((tm, d), lambda i, offs: (i, 0)),
        scratch_shapes=[pltpu.VMEM((tm, d), F32), pltpu.VMEM((tm, d), F32), pltpu.SemaphoreType.DMA(())])
    return pl.pallas_call(
        _combine_kernel,
        grid_spec=grid_spec,
        out_shape=jax.ShapeDtypeStruct((t, d), F32),
        compiler_params=_params(("arbitrary",)),
        name="moe_combine",
    )(offs, ri, y, rw, ys)


def _moe(y, h, rw, ri, counts, wg_bf, wu_bf, wd_bf, tm_e):
    t = y.shape[0]
    n_tiles = 2 * t // tm_e + N_EXPERTS
    tiles_per = (counts + tm_e - 1) // tm_e
    tile_end = jnp.cumsum(tiles_per)
    offs = ((tile_end - tiles_per) * tm_e).astype(I32)
    n_used = tile_end[-1].astype(I32)
    tile_id = jnp.minimum(jnp.arange(n_tiles, dtype=I32), n_used - 1)
    tile_exp = jnp.sum(tile_id[:, None] >= tile_end[None, :], axis=1).astype(I32)
    hs = _dispatch(h, ri, offs, n_tiles * tm_e)
    ys = _experts(hs, wg_bf, wu_bf, wd_bf, tile_id, tile_exp, n_used.reshape(1), tm_e)
    return _combine(y, rw, ri, offs, ys)


def kernel(x_prompt, x_sample, cache_kv_a, cache_kv_b1, cache_kv_b2, cache_kv_b3, page_table, g_attn, g_ffn, w_qkv_a, w_o_a, sb_bias, g_kv, w_kv_b, k_norm_b, w_q_b, q_norm_b, w_o_b, w_router_group, b_router_group, w_router_expert, b_router_expert, w_gate, w_up, w_down):
    b, s, d = x_prompt.shape
    db, t, _ = x_sample.shape
    assert t == 1
    n_a, n_pool, page, _, n_heads_a, _ = cache_kv_a.shape
    depth = g_attn.shape[0]
    n_kv = cache_kv_b1.shape[3]
    n_heads_b = w_q_b.shape[-1] // (N_DIL * HEAD_DIM)
    hd_a = n_heads_a * HEAD_DIM
    past_len = page_table.shape[1] * page
    buffers = (cache_kv_b1, cache_kv_b2, cache_kv_b3)
    tm_p, tm_s = 256, db * t
    tme_p, tme_s = 256, 32

    xp = x_prompt.reshape(b * s, d)
    xs = x_sample.reshape(db * t, d)
    wg_bf, wu_bf, wd_bf = w_gate.astype(BF16), w_up.astype(BF16), w_down.astype(BF16)
    new_a_p, new_a_s = [], []
    kv_p = kv_s = None
    for layer in range(depth):
        if layer < n_a:
            i = layer
            wqkv = w_qkv_a[i].astype(BF16)
            wo = w_o_a[i].astype(BF16)
            qkv_p = _proj_a(xp, g_attn[layer], wqkv, tm_p)
            qkv_s = _proj_a(xs, g_attn[layer], wqkv, tm_s)
            o_p = _sb_prompt(qkv_p.reshape(b, s, 3 * hd_a), sb_bias[i], n_heads_a)
            o_s = _sb_sample(qkv_s.reshape(db, 1, 3 * hd_a), cache_kv_a[i].reshape(n_pool, page, 2 * hd_a),
                             page_table, sb_bias[i], n_heads_a)
            attn_p = (o_p.reshape(b * s, hd_a),)
            attn_s = (o_s.reshape(db, hd_a),)
            new_a_p.append(qkv_p[:, hd_a:].reshape(b, s, 2, n_heads_a, HEAD_DIM))
            new_a_s.append(qkv_s[:, hd_a:].reshape(db, t, 2, n_heads_a, HEAD_DIM))
        else:
            j = layer - n_a
            wo = w_o_b[j].astype(BF16)
            wq = w_q_b[j].astype(BF16)
            wkv = w_kv_b.astype(BF16)
            tab_p = _rope_tables(jnp.arange(s))
            tab_s = _rope_tables(jnp.full((tm_s,), past_len))
            kv_pj, q_p = _proj_b(xp, g_kv, wkv, g_attn[layer], wq, k_norm_b, q_norm_b[j], tab_p, tm_p,
                                 n_heads_b, n_kv)
            kv_sj, q_s = _proj_b(xs, g_kv, wkv, g_attn[layer], wq, k_norm_b, q_norm_b[j], tab_s, tm_s,
                                 n_heads_b, n_kv)
            if j == 0:
                kv_p, kv_s = kv_pj, kv_sj
            outs_p, lses_p, outs_s, lses_s = [], [], [], []
            for g in range(N_DIL):
                o, l = _dil_prompt(q_p.reshape(b, s, -1), kv_p.reshape(b, s, -1), g, n_heads_b, n_kv)
                outs_p.append(o)
                lses_p.append(l)
                o, l = _dil_sample(q_s.reshape(db, 1, -1), buffers[g].reshape(db, buffers[g].shape[1], -1),
                                   kv_s.reshape(db, 1, -1), g, n_heads_b, n_kv)
                outs_s.append(o)
                lses_s.append(l)
            attn_p = tuple(outs_p) + tuple(lses_p)
            attn_s = tuple(outs_s) + tuple(lses_s)
        router = (g_ffn[layer], w_router_group[layer], b_router_group[layer],
                  w_router_expert[layer], b_router_expert[layer])
        experts = (wg_bf[layer], wu_bf[layer], wd_bf[layer])
        xp = _moe(*_oproj_router(xp, attn_p, wo, *router, tm_p), *experts, tme_p)
        xs = _moe(*_oproj_router(xs, attn_s, wo, *router, tm_s), *experts, tme_s)

    kv_p5 = kv_p.reshape(b, s, N_DIL, 2, n_kv, HEAD_DIM)
    kv_s5 = kv_s.reshape(db, t, N_DIL, 2, n_kv, HEAD_DIM)
    outs = [xp.reshape(b, s, d), xs.reshape(db, t, d), jnp.stack(new_a_p), jnp.stack(new_a_s)]
    for g, (window, _) in enumerate(DILATED):
        outs.append(kv_p5[:, s - min(window, s):, g])
        outs.append(kv_s5[:, :, g])
    return tuple(outs)
```

```python
import functools

import numpy as np
import jax
import jax.numpy as jnp
from jax import lax
from jax.experimental import pallas as pl
from jax.experimental.pallas import tpu as pltpu

F32 = jnp.float32
BF16 = jnp.bfloat16
I32 = jnp.int32

HEAD_DIM = 64
ROT_DIM = HEAD_DIM // 4
ROPE_THETA = 500000.0
DILATED = ((128, 1), (512, 4), (2048, 16))
N_DIL = len(DILATED)
BLOCK = 128
N_GROUPS = 4
EXPERTS_PER_GROUP = 8
N_EXPERTS = N_GROUPS * EXPERTS_PER_GROUP
RMS_EPS = 1e-6
NEG_INF = -1e30

LANES = 128
SUBLANES = 8
VMEM_LIMIT = 48 * 1024 * 1024
GROUP_LANE0 = N_EXPERTS


def _dot(a, b):
    return jnp.dot(a, b, preferred_element_type=F32)


def _dot_nt(a, b):
    return lax.dot_general(a, b, (((1,), (1,)), ((), ())), preferred_element_type=F32)


def _split2(x):
    hi = x.astype(BF16)
    lo = (x - hi.astype(F32)).astype(BF16)
    return hi, lo


def _split3(x):
    hi = x.astype(BF16)
    r = x - hi.astype(F32)
    mid = r.astype(BF16)
    lo = (r - mid.astype(F32)).astype(BF16)
    return hi, mid, lo


def _rms_scale(x):
    return x * lax.rsqrt(jnp.mean(x * x, axis=-1, keepdims=True) + RMS_EPS)


def _params(sem):
    return pltpu.CompilerParams(dimension_semantics=sem, vmem_limit_bytes=VMEM_LIMIT)


def _proj_a_kernel(x_ref, g_ref, w_ref, o_ref, *, tn):
    xn = (_rms_scale(x_ref[...]) * g_ref[...]).astype(BF16)
    for c in range(o_ref.shape[-1] // tn):
        o_ref[:, c * tn:(c + 1) * tn] = _dot(xn, w_ref[:, c * tn:(c + 1) * tn])


def _proj_a(x, g, w_bf, tm):
    t, d = x.shape
    n = w_bf.shape[1]
    return pl.pallas_call(
        functools.partial(_proj_a_kernel, tn=512),
        grid=(t // tm,),
        in_specs=[pl.BlockSpec((tm, d), lambda i: (i, 0)),
                  pl.BlockSpec((1, d), lambda i: (0, 0)),
                  pl.BlockSpec((d, n), lambda i: (0, 0))],
        out_specs=pl.BlockSpec((tm, n), lambda i: (i, 0)),
        out_shape=jax.ShapeDtypeStruct((t, n), F32),
        compiler_params=_params(("parallel",)),
        name="proj_a",
    )(x, g.reshape(1, d), w_bf)


def _headnorm_rope(y, gain, bd, c, s1, s2):
    hi, lo = _split2(y * y)
    ms = (_dot(hi, bd) + _dot(lo, bd)) * (1.0 / HEAD_DIM)
    yn = y * lax.rsqrt(ms + RMS_EPS) * gain
    half = ROT_DIM // 2
    return yn * c + pltpu.roll(yn, LANES - half, 1) * s1 + pltpu.roll(yn, half, 1) * s2


def _proj_b_kernel(x_ref, gkv_ref, wkv_ref, gq_ref, wq_ref, gainkv_ref, gainq_ref, bd_ref,
                   c_ref, s1_ref, s2_ref, kv_ref, q_ref, *, tn):
    xh = _rms_scale(x_ref[...])
    bd = bd_ref[...]
    c, s1, s2 = c_ref[...], s1_ref[...], s2_ref[...]
    kv_cols = 2 * (kv_ref.shape[-1] // (2 * N_DIL))

    def run(g_ref, w_ref, gain_ref, o_ref, is_key):
        xn = (xh * g_ref[...]).astype(BF16)
        for cc in range(o_ref.shape[-1] // tn):
            y = _dot(xn, w_ref[:, cc * tn:(cc + 1) * tn])
            for s in range(tn // LANES):
                col = cc * tn + s * LANES
                ys = y[:, s * LANES:(s + 1) * LANES]
                if is_key(col):
                    ys = _headnorm_rope(ys, gain_ref[:, col:col + LANES], bd, c, s1, s2)
                o_ref[:, col:col + LANES] = ys

    run(gkv_ref, wkv_ref, gainkv_ref, kv_ref, lambda col: (col % kv_cols) < kv_cols // 2)
    run(gq_ref, wq_ref, gainq_ref, q_ref, lambda col: True)


def _rope_tables(pos):
    half = ROT_DIM // 2
    inv = jnp.power(ROPE_THETA, -jnp.arange(half, dtype=F32) * 2.0 / ROT_DIM)
    ang = pos.astype(F32)[:, None] * inv[None, :]
    cos, sin = jnp.cos(ang), jnp.sin(ang)
    n = pos.shape[0]
    ones = jnp.ones((n, HEAD_DIM - ROT_DIM), F32)
    zeros = jnp.zeros((n, HEAD_DIM - ROT_DIM), F32)
    zh = jnp.zeros((n, half), F32)
    c = jnp.concatenate([cos, cos, ones], axis=1)
    s1 = jnp.concatenate([-sin, zh, zeros], axis=1)
    s2 = jnp.concatenate([zh, sin, zeros], axis=1)
    rep = LANES // HEAD_DIM
    return tuple(jnp.tile(a, (1, rep)) for a in (c, s1, s2))


def _proj_b(x, g_kv, wkv_bf, g_q, wq_bf, k_norm, q_norm, tables, tm, n_heads_q, n_kv):
    t, d = x.shape
    nkv, nq = wkv_bf.shape[1], wq_bf.shape[1]
    gain_kv = jnp.broadcast_to(k_norm[:, None, None, :], (N_DIL, 2, n_kv, HEAD_DIM)).reshape(1, nkv)
    gain_q = jnp.broadcast_to(q_norm[:, None, :], (N_DIL, n_heads_q, HEAD_DIM)).reshape(1, nq)
    lane = np.arange(LANES)
    bd = jnp.asarray((lane[:, None] // HEAD_DIM == lane[None, :] // HEAD_DIM), BF16)
    n_tab = tables[0].shape[0] // tm
    tab_spec = pl.BlockSpec((tm, LANES), lambda i: (i % n_tab, 0))
    full = lambda shape: pl.BlockSpec(shape, lambda i: (0, 0))
    return pl.pallas_call(
        functools.partial(_proj_b_kernel, tn=512),
        grid=(t // tm,),
        in_specs=[pl.BlockSpec((tm, d), lambda i: (i, 0)),
                  full((1, d)), full((d, nkv)), full((1, d)), full((d, nq)),
                  full((1, nkv)), full((1, nq)), full((LANES, LANES)),
                  tab_spec, tab_spec, tab_spec],
        out_specs=[pl.BlockSpec((tm, nkv), lambda i: (i, 0)),
                   pl.BlockSpec((tm, nq), lambda i: (i, 0))],
        out_shape=[jax.ShapeDtypeStruct((t, nkv), F32), jax.ShapeDtypeStruct((t, nq), F32)],
        compiler_params=_params(("parallel",)),
        name="proj_b",
    )(x, g_kv.reshape(1, d), wkv_bf, g_q.reshape(1, d), wq_bf, gain_kv, gain_q, bd, *tables)


def _tree_prod(xs):
    while len(xs) > 1:
        xs = [xs[i] * xs[i + 1] for i in range(0, len(xs) - 1, 2)] + ([xs[-1]] if len(xs) % 2 else [])
    return xs[0]


def _sb_prompt_kernel(bias_ref, pm_ref, q_ref, k_ref, v_ref, o_ref, kbf, vtb, acc, zbuf, abuf, *, blk,
                      scan_blocks):
    hp = pl.program_id(1)
    qi = pl.program_id(2)
    seg = blk // SUBLANES
    per = seg // scan_blocks
    wide = 2 * blk
    n_blocks = kbf.shape[0]

    @pl.when(qi == 0)
    def _():
        pm = pm_ref[...]

        def prep(j, carry):
            rows = pl.ds(pl.multiple_of(j * blk, blk), blk)
            kbf[j] = _dot(pm, k_ref[rows, :].astype(BF16)).astype(BF16)
            vtb[j] = _dot(pm, v_ref[rows, :].astype(BF16)).T.astype(BF16)
            return carry

        lax.fori_loop(0, n_blocks, prep, 0)

    q = q_ref[...] * (HEAD_DIM ** -0.5)
    lane = lax.broadcasted_iota(I32, q.shape, 1)
    q2 = jnp.concatenate([jnp.where(lane < HEAD_DIM, q, 0.0), jnp.where(lane >= HEAD_DIM, q, 0.0)],
                         axis=0).astype(BF16)
    lane_w = lax.broadcasted_iota(I32, (1, wide), 1)
    bias = jnp.where(lane_w < blk, bias_ref[2 * hp], bias_ref[2 * hp + 1])

    sub = lax.broadcasted_iota(I32, (SUBLANES, wide), 0)

    def scores(j):
        return _dot_nt(kbf[j], q2) + bias

    def gates(z, masked):
        e = jnp.exp(-jnp.abs(z))
        r = 1.0 / (1.0 + e)
        s = e * r
        pos = z >= 0.0
        beta = jnp.where(pos, r, s)
        om = jnp.where(pos, s, r)
        if masked:
            rho = lax.broadcasted_iota(I32, (blk, wide), 0)
            key_idx = (rho & (SUBLANES - 1)) * seg + (rho >> 3)
            query_idx = lax.broadcasted_iota(I32, (blk, wide), 1) & (blk - 1)
            diag = key_idx < query_idx
            beta = jnp.where(diag, beta, 0.0)
            om = jnp.where(diag, om, 1.0)
        return beta, om

    def scan(beta, om, carry):
        oms = [om[v * SUBLANES:(v + 1) * SUBLANES, :] for v in range(seg)]
        chain_tot = [_tree_prod(oms[c * per:(c + 1) * per]) for c in range(scan_blocks)]
        inc = _tree_prod(chain_tot)
        for d in (1, 2, 4):
            inc = inc * jnp.where(sub + d < SUBLANES, pltpu.roll(inc, SUBLANES - d, 0), 1.0)
        exc = jnp.where(sub + 1 < SUBLANES, pltpu.roll(inc, SUBLANES - 1, 0), 1.0)
        new_carry = inc[0:1, :] * carry
        start = [None] * scan_blocks
        start[scan_blocks - 1] = exc * carry
        for c in range(scan_blocks - 2, -1, -1):
            start[c] = start[c + 1] * chain_tot[c + 1]
        rows = [None] * seg
        for c in range(scan_blocks):
            run = start[c]
            for v in range((c + 1) * per - 1, c * per - 1, -1):
                rows[v] = beta[v * SUBLANES:(v + 1) * SUBLANES, :] * run
                if v > c * per:
                    run = run * oms[v]
        return jnp.concatenate(rows, axis=0).astype(BF16), new_carry

    ones = jnp.ones((SUBLANES, wide), F32)
    abuf[...], carry = scan(*gates(scores(qi), True), ones)
    zbuf[...] = scores(jnp.maximum(qi - 1, 0))
    acc[...] = jnp.zeros_like(acc)

    def body(t, carry):
        j = qi - 1 - t
        acc[...] += _dot(vtb[j + 1], abuf[...])
        beta, om = gates(zbuf[...], False)
        zbuf[...] = scores(jnp.maximum(j - 1, 0))
        abuf[...], carry = scan(beta, om, carry)
        return carry

    lax.fori_loop(0, qi, body, carry)
    acc[...] += _dot(vtb[0], abuf[...])
    row = lax.broadcasted_iota(I32, (LANES, blk), 0)
    o_ref[...] = jnp.where(row < HEAD_DIM, acc[:, :blk], acc[:, blk:]).T


def _sb_prompt(qkv, bias, n_heads, blk=256):
    b, s, _ = qkv.shape
    n_hp = n_heads * HEAD_DIM // LANES
    n_blocks = s // blk
    seg = blk // SUBLANES
    rho = np.arange(blk)
    pm = jnp.asarray(((rho % SUBLANES) * seg + rho // SUBLANES)[:, None] == np.arange(blk)[None, :], BF16)
    return pl.pallas_call(
        functools.partial(_sb_prompt_kernel, blk=blk, scan_blocks=4),
        grid=(b, n_hp, n_blocks),
        in_specs=[pl.BlockSpec(memory_space=pltpu.SMEM),
                  pl.BlockSpec((blk, blk), lambda bi, hp, qi: (0, 0)),
                  pl.BlockSpec((None, blk, LANES), lambda bi, hp, qi: (bi, qi, hp)),
                  pl.BlockSpec((None, s, LANES), lambda bi, hp, qi: (bi, 0, n_hp + hp)),
                  pl.BlockSpec((None, s, LANES), lambda bi, hp, qi: (bi, 0, 2 * n_hp + hp))],
        out_specs=pl.BlockSpec((None, blk, LANES), lambda bi, hp, qi: (bi, qi, hp)),
        out_shape=jax.ShapeDtypeStruct((b, s, n_heads * HEAD_DIM), F32),
        scratch_shapes=[pltpu.VMEM((n_blocks, blk, LANES), BF16),
                        pltpu.VMEM((n_blocks, LANES, blk), BF16),
                        pltpu.VMEM((LANES, 2 * blk), F32),
                        pltpu.VMEM((blk, 2 * blk), F32),
                        pltpu.VMEM((blk, 2 * blk), BF16)],
        compiler_params=_params(("parallel", "parallel", "arbitrary")),
        name="sb_prompt",
    )(bias, pm, qkv, qkv, qkv)


def _softplus(x):
    return jnp.maximum(x, 0.0) + jnp.log1p(jnp.exp(-jnp.abs(x)))


def _dot3(x, w):
    return sum(_dot(part, w) for part in _split3(x))


def _sb_sample_kernel(pt_ref, q_ref, *refs, nseq):
    kv_refs = refs[:nseq]
    bias_ref, sel_ref, within_ref, same_ref, later_ref, seq_ref, o_ref, carry, acc = refs[nseq:]
    p = pl.program_id(1)
    page, _, n_heads, _ = kv_refs[0].shape
    n_rows = page * n_heads
    n_tiles = n_rows // LANES

    @pl.when(p == 0)
    def _():
        carry[...] = jnp.zeros_like(carry)
        acc[...] = jnp.zeros_like(acc)

    sel = sel_ref[...]
    z_rows, values = [], []
    for s in range(nseq):
        q = (q_ref[s] * (HEAD_DIM ** -0.5)).astype(BF16)
        kk = kv_refs[s][:, 0].reshape(n_rows, HEAD_DIM).astype(BF16)
        values.append(kv_refs[s][:, 1].reshape(n_rows, HEAD_DIM).astype(BF16))
        zt = _dot_nt(q, kk)
        z_rows += [jnp.sum(zt[:, c * LANES:(c + 1) * LANES] * sel, axis=0, keepdims=True)
                   for c in range(n_tiles)]
    z = jnp.concatenate(z_rows, axis=0) + bias_ref[...]
    log_rest = -_softplus(z)
    log_beta = log_rest + z
    tile_tot = _dot3(log_rest, same_ref[...])
    tot3 = _split3(tile_tot)
    later = sum(_dot(later_ref[...], part) for part in tot3)
    tail = _dot3(log_rest, within_ref[...]) + later + carry[...]
    a = jnp.exp(log_beta + tail)
    carry[...] += sum(_dot(seq_ref[...], part) for part in tot3)
    for s in range(nseq):
        a_rows = jnp.concatenate([a[s * n_tiles + c:s * n_tiles + c + 1, :] * sel for c in range(n_tiles)],
                                 axis=1)
        acc[s] += _dot(a_rows.astype(BF16), values[s])

    @pl.when(p == pl.num_programs(1) - 1)
    def _():
        o_ref[...] = acc[...]


def _sb_sample(q, cache, page_table, bias, nseq=4):
    db, n_pages = page_table.shape
    _, page, _, n_heads, _ = cache.shape
    n_tiles = page * n_heads // LANES
    lane = np.arange(LANES)
    head, key = lane % n_heads, lane // n_heads
    sel = jnp.asarray(np.arange(n_heads)[:, None] == head[None, :], F32)
    same = jnp.asarray(head[:, None] == head[None, :], BF16)
    within = jnp.asarray((head[:, None] == head[None, :]) & (key[:, None] > key[None, :]), BF16)
    row = np.arange(nseq * n_tiles)
    same_seq = row[:, None] // n_tiles == row[None, :] // n_tiles
    later = jnp.asarray(same_seq & (row[None, :] > row[:, None]), BF16)
    seq = jnp.asarray(same_seq, BF16)
    bias_l = bias[head].reshape(1, LANES)
    const = lambda shape: pl.BlockSpec(shape, lambda b, p, pt: (0, 0))
    page_spec = lambda s: pl.BlockSpec((None, page, 2, n_heads, HEAD_DIM),
                                       lambda b, p, pt: (pt[b * nseq + s, n_pages - 1 - p], 0, 0, 0, 0))
    grid_spec = pltpu.PrefetchScalarGridSpec(
        num_scalar_prefetch=1,
        grid=(db // nseq, n_pages),
        in_specs=[pl.BlockSpec((nseq, n_heads, HEAD_DIM), lambda b, p, pt: (b, 0, 0))]
                 + [page_spec(s) for s in range(nseq)]
                 + [const((1, LANES)), const((n_heads, LANES)), const((LANES, LANES)), const((LANES, LANES)),
                    const((nseq * n_tiles, nseq * n_tiles)), const((nseq * n_tiles, nseq * n_tiles))],
        out_specs=pl.BlockSpec((nseq, n_heads, HEAD_DIM), lambda b, p, pt: (b, 0, 0)),
        scratch_shapes=[pltpu.VMEM((nseq * n_tiles, LANES), F32), pltpu.VMEM((nseq, n_heads, HEAD_DIM), F32)])
    return pl.pallas_call(
        functools.partial(_sb_sample_kernel, nseq=nseq),
        grid_spec=grid_spec,
        out_shape=jax.ShapeDtypeStruct((db, n_heads, HEAD_DIM), F32),
        compiler_params=_params(("parallel", "arbitrary")),
        name="sb_sample",
    )(page_table, q, *([cache] * nseq), bias_l, sel, within, same, later, seq)


def _dil_prompt_kernel(q_ref, kp_ref, kc_ref, vp_ref, vc_ref, rep_ref, o_ref, lse_ref, *, steps, n_kv):
    i = pl.program_id(2)
    blk = q_ref.shape[0]
    width = kc_ref.shape[-1]
    per_kv = q_ref.shape[-1] // width
    k2 = jnp.concatenate([kp_ref[...], kc_ref[...]], axis=0).astype(BF16)
    v2 = jnp.concatenate([vp_ref[...], vc_ref[...]], axis=0).astype(BF16)
    n_heads = n_kv * per_kv
    row = lax.broadcasted_iota(I32, (n_heads * blk, 2 * blk), 0) & (blk - 1)
    col = lax.broadcasted_iota(I32, (n_heads * blk, 2 * blk), 1)
    rel = blk + row - col
    first_col = jnp.where(i > 0, 0, blk)
    mask = (rel >= 0) & (rel <= steps) & (col >= first_col)
    lane = lax.broadcasted_iota(I32, (blk, width), 1)
    sels = [(lane >= j * HEAD_DIM) & (lane < (j + 1) * HEAD_DIM) for j in range(per_kv)]
    scale = HEAD_DIM ** -0.5
    s_parts, v_reps = [], []
    for c in range(n_kv):
        rep = rep_ref[c]
        k_rep = _dot(k2, rep).astype(BF16)
        v_reps.append(_dot(v2, rep).astype(BF16))
        qc = q_ref[:, c * width:(c + 1) * width] * scale
        qs = jnp.concatenate([jnp.where(sel, qc, 0.0) for sel in sels], axis=0).astype(BF16)
        s_parts.append(_dot_nt(qs, k_rep))
    s = jnp.where(mask, jnp.concatenate(s_parts, axis=0), NEG_INF)
    m = jnp.max(s, axis=-1, keepdims=True)
    pexp = jnp.exp(s - m)
    den = jnp.sum(pexp, axis=-1, keepdims=True)
    lse = m + jnp.log(den)
    p = (pexp * (1.0 / den)).astype(BF16)
    for c in range(n_kv):
        oc = _dot(p[c * per_kv * blk:(c + 1) * per_kv * blk, :], v_reps[c])
        o_c = jnp.zeros((blk, width), F32)
        for j, sel in enumerate(sels):
            o_c = o_c + jnp.where(sel, oc[j * blk:(j + 1) * blk, :], 0.0)
        o_ref[:, c * width:(c + 1) * width] = o_c
    lane_l = lax.broadcasted_iota(I32, (blk, LANES), 1)
    lse_tile = jnp.zeros((blk, LANES), F32)
    for h in range(n_heads):
        lse_tile = lse_tile + jnp.where(lane_l == h, lse[h * blk:(h + 1) * blk, :], 0.0)
    lse_ref[...] = lse_tile


def _rep_matrices(n_kv):
    width = n_kv * HEAD_DIM
    src = np.arange(width)[:, None]
    dst = np.arange(width)[None, :]
    return jnp.asarray(np.stack([(src == c * HEAD_DIM + dst % HEAD_DIM) for c in range(n_kv)]), BF16)


def _dil_prompt(q, kv, g, n_heads, n_kv):
    b, s, nq = q.shape
    nkv = kv.shape[-1]
    window, dil = DILATED[g]
    steps = window // dil
    hd = n_heads * HEAD_DIM
    width = n_kv * HEAD_DIM
    n_sub = s // dil
    nb = n_sub // BLOCK
    qv = q.reshape(b, n_sub, dil * nq)
    kvv = kv.reshape(b, n_sub, dil * nkv)
    q_blocks = nq // hd
    kv_blocks = nkv // width
    kcol = lambda r: r * kv_blocks + g * 2
    out = pl.pallas_call(
        functools.partial(_dil_prompt_kernel, steps=steps, n_kv=n_kv),
        grid=(b, dil, nb),
        in_specs=[pl.BlockSpec((None, BLOCK, hd), lambda bi, r, i: (bi, i, r * q_blocks + g)),
                  pl.BlockSpec((None, BLOCK, width), lambda bi, r, i: (bi, jnp.maximum(i - 1, 0), kcol(r))),
                  pl.BlockSpec((None, BLOCK, width), lambda bi, r, i: (bi, i, kcol(r))),
                  pl.BlockSpec((None, BLOCK, width), lambda bi, r, i: (bi, jnp.maximum(i - 1, 0), kcol(r) + 1)),
                  pl.BlockSpec((None, BLOCK, width), lambda bi, r, i: (bi, i, kcol(r) + 1)),
                  pl.BlockSpec((n_kv, width, width), lambda bi, r, i: (0, 0, 0))],
        out_specs=[pl.BlockSpec((None, BLOCK, hd), lambda bi, r, i: (bi, i, r)),
                   pl.BlockSpec((None, BLOCK, LANES), lambda bi, r, i: (bi, i, r))],
        out_shape=[jax.ShapeDtypeStruct((b, n_sub, dil * hd), F32),
                   jax.ShapeDtypeStruct((b, n_sub, dil * LANES), F32)],
        compiler_params=_params(("parallel", "parallel", "parallel")),
        name=f"dil_prompt_{g}",
    )(qv, kvv, kvv, kvv, kvv, _rep_matrices(n_kv))
    return out[0].reshape(b * s, hd), out[1].reshape(b * s, LANES)


def _dil_sample_kernel(q_ref, buf_ref, kvn_ref, fold_ref, unfold_ref, hmask_ref, o_ref, lse_ref, *, nseq, n_valid):
    width = kvn_ref.shape[-1] // 2
    n_heads, hd = hmask_ref.shape
    n_keys = buf_ref.shape[1]
    hmask = hmask_ref[...]
    fold = fold_ref[...]
    unfold = unfold_ref[...]
    scale = HEAD_DIM ** -0.5
    key_ok = lax.broadcasted_iota(I32, (n_heads, n_keys), 1) >= n_keys - n_valid
    eye = (lax.broadcasted_iota(I32, (n_heads, LANES), 0) == lax.broadcasted_iota(I32, (n_heads, LANES), 1))
    for n in range(nseq):
        q_rows = (hmask * (q_ref[n] * scale)).astype(BF16)
        qm = _dot(q_rows, fold)
        k = buf_ref[n, :, :width].astype(BF16)
        v = buf_ref[n, :, width:].astype(BF16)
        k_new = kvn_ref[n, :, :width]
        v_new = kvn_ref[n, :, width:]
        s = jnp.where(key_ok, _dot_nt(qm.astype(BF16), k), NEG_INF)
        s_new = jnp.sum(qm * k_new, axis=-1, keepdims=True)
        m = jnp.maximum(jnp.max(s, axis=-1, keepdims=True), s_new)
        den = jnp.sum(jnp.exp(s - m), axis=-1, keepdims=True) + jnp.exp(s_new - m)
        lse = m + jnp.log(den)
        p = jnp.exp(s - lse)
        p_new = jnp.exp(s_new - lse)
        o = _dot(p.astype(BF16), v) + p_new * v_new
        o_full = _dot(o.astype(BF16), unfold) * hmask
        o_ref[n] = jnp.sum(o_full, axis=0, keepdims=True)
        lse_ref[n] = jnp.sum(jnp.where(eye, lse, 0.0), axis=0, keepdims=True)


def _dil_sample(q3, buf, kv3, g, n_heads, n_kv, nseq=8):
    db = q3.shape[0]
    window, dil = DILATED[g]
    steps = window // dil
    hd = n_heads * HEAD_DIM
    width = n_kv * HEAD_DIM
    lb = buf.shape[1]
    n_keys = -(-lb // dil)
    first = lb - n_keys * dil
    n_valid = min(steps, lb // dil)
    if first < 0:
        buf = jnp.pad(buf, ((0, 0), (-first, 0), (0, 0)))
    bufv = buf.reshape(db, n_keys, dil * 2 * width)
    per_kv = n_heads // n_kv
    src = np.arange(hd)[:, None]
    dst = np.arange(width)[None, :]
    fold_np = (src // (HEAD_DIM * per_kv) == dst // HEAD_DIM) & (src % HEAD_DIM == dst % HEAD_DIM)
    fold = jnp.asarray(fold_np, BF16)
    unfold = jnp.asarray(fold_np.T, BF16)
    hmask = jnp.asarray(np.arange(hd)[None, :] // HEAD_DIM == np.arange(n_heads)[:, None], F32)
    full2 = lambda shape: pl.BlockSpec(shape, lambda i: (0, 0))
    o, lse = pl.pallas_call(
        functools.partial(_dil_sample_kernel, nseq=nseq, n_valid=n_valid),
        grid=(db // nseq,),
        in_specs=[pl.BlockSpec((nseq, 1, hd), lambda i: (i, 0, g)),
                  pl.BlockSpec((nseq, n_keys, 2 * width), lambda i: (i, 0, 0)),
                  pl.BlockSpec((nseq, 1, 2 * width), lambda i: (i, 0, g)),
                  full2((hd, width)), full2((width, hd)), full2((n_heads, hd))],
        out_specs=[pl.BlockSpec((nseq, 1, hd), lambda i: (i, 0, 0)),
                   pl.BlockSpec((nseq, 1, LANES), lambda i: (i, 0, 0))],
        out_shape=[jax.ShapeDtypeStruct((db, 1, hd), F32), jax.ShapeDtypeStruct((db, 1, LANES), F32)],
        compiler_params=_params(("parallel",)),
        name=f"dil_sample_{g}",
    )(q3, bufv, kv3, fold, unfold, hmask)
    return o.reshape(db, hd), lse.reshape(db, LANES)


def _route(h, wr_ref, br_ref, ltri_ref, carry_ref):
    tm = h.shape[0]
    h3 = _split3(h)
    w3 = _split3(wr_ref[...])
    logits = br_ref[...]
    for a, b in ((0, 0), (0, 1), (1, 0), (1, 1), (0, 2), (2, 0)):
        logits = logits + _dot(h3[a], w3[b])
    lane = lax.broadcasted_iota(I32, (tm, LANES), 1)
    lane_f = lane.astype(F32)
    big = float(LANES)
    is_group = (lane >= GROUP_LANE0) & (lane < GROUP_LANE0 + N_GROUPS)
    gl = jnp.where(is_group, logits, NEG_INF)
    gmax = jnp.max(gl, axis=-1, keepdims=True)
    gate = 1.0 / jnp.sum(jnp.exp(gl - gmax), axis=-1, keepdims=True)
    gidx = jnp.min(jnp.where(gl == gmax, lane_f, big), axis=-1, keepdims=True) - GROUP_LANE0
    lo = gidx * EXPERTS_PER_GROUP
    in_group = (lane_f >= lo) & (lane_f < lo + EXPERTS_PER_GROUP)
    el = jnp.where(in_group, logits, NEG_INF)
    m1 = jnp.max(el, axis=-1, keepdims=True)
    i1 = jnp.min(jnp.where(el == m1, lane_f, big), axis=-1, keepdims=True)
    el2 = jnp.where(lane_f == i1, NEG_INF, el)
    m2 = jnp.max(el2, axis=-1, keepdims=True)
    i2 = jnp.min(jnp.where(el2 == m2, lane_f, big), axis=-1, keepdims=True)
    t = jnp.exp(m2 - m1)
    w1 = gate / (1.0 + t)
    w2 = gate * t / (1.0 + t)
    hit1 = lane_f == i1
    hit2 = lane_f == i2
    onehot = (hit1 | hit2).astype(F32)
    ranks = _dot(ltri_ref[...], onehot.astype(BF16)) + carry_ref[0:1, :]
    r1 = jnp.sum(jnp.where(hit1, ranks, 0.0), axis=-1, keepdims=True)
    r2 = jnp.sum(jnp.where(hit2, ranks, 0.0), axis=-1, keepdims=True)
    carry_ref[...] += jnp.sum(onehot, axis=0, keepdims=True)
    ints = (jnp.where(lane == 0, i1, 0.0) + jnp.where(lane == 1, i2, 0.0)
            + jnp.where(lane == 2, r1, 0.0) + jnp.where(lane == 3, r2, 0.0))
    wts = jnp.where(lane == 0, w1, 0.0) + jnp.where(lane == 1, w2, 0.0)
    return ints, wts


def _oproj_router_kernel(*refs, combine):
    if combine:
        (x_ref, o1_ref, o2_ref, o3_ref, l1_ref, l2_ref, l3_ref, exp_ref, wo_ref, g_ref, wr_ref, br_ref,
         ltri_ref, y_ref, h_ref, rw_ref, ri_ref, cnt_ref, carry_ref) = refs
        l1, l2, l3 = l1_ref[...], l2_ref[...], l3_ref[...]
        m = jnp.maximum(jnp.maximum(l1, l2), l3)
        e1, e2, e3 = jnp.exp(l1 - m), jnp.exp(l2 - m), jnp.exp(l3 - m)
        inv = 1.0 / (e1 + e2 + e3)
        ex = exp_ref[...]
        o = jnp.zeros(o1_ref.shape, F32)
        for e, o_ref in ((e1, o1_ref), (e2, o2_ref), (e3, o3_ref)):
            hi, lo = _split2(e * inv)
            o = o + (_dot(hi, ex) + _dot(lo, ex)) * o_ref[...]
    else:
        (x_ref, o_ref, wo_ref, g_ref, wr_ref, br_ref, ltri_ref,
         y_ref, h_ref, rw_ref, ri_ref, cnt_ref, carry_ref) = refs
        o = o_ref[...]

    @pl.when(pl.program_id(0) == 0)
    def _():
        carry_ref[...] = jnp.zeros_like(carry_ref)

    y = x_ref[...] + _dot(o.astype(BF16), wo_ref[...])
    y_ref[...] = y
    h = _rms_scale(y) * g_ref[...]
    h_ref[...] = h
    ints, wts = _route(h, wr_ref, br_ref, ltri_ref, carry_ref)
    rw_ref[...] = wts
    ri_ref[...] = ints.T[0:SUBLANES, :].astype(I32)
    cnt_ref[...] = carry_ref[...]


def _oproj_router(x, attn, wo_bf, g_ffn, w_rg, b_rg, w_re, b_re, tm):
    t, d = x.shape
    hd = wo_bf.shape[0]
    nt = t // tm
    wr = jnp.zeros((d, LANES), F32).at[:, :N_EXPERTS].set(w_re).at[:, GROUP_LANE0:GROUP_LANE0 + N_GROUPS].set(w_rg)
    br = jnp.zeros((1, LANES), F32).at[0, :N_EXPERTS].set(b_re).at[0, GROUP_LANE0:GROUP_LANE0 + N_GROUPS].set(b_rg)
    ltri = jnp.asarray(np.arange(tm)[:, None] > np.arange(tm)[None, :], BF16)
    row = lambda n: pl.BlockSpec((tm, n), lambda i: (i, 0))
    full = lambda shape: pl.BlockSpec(shape, lambda i: (0, 0))
    combine = len(attn) > 1
    if combine:
        ex = jnp.asarray(np.arange(LANES)[:, None] == np.arange(hd)[None, :] // HEAD_DIM, BF16)
        attn_specs = [row(hd)] * 3 + [row(LANES)] * 3 + [full((LANES, hd))]
        attn_args = list(attn) + [ex]
    else:
        attn_specs = [row(hd)]
        attn_args = list(attn)
    y, h, rw, ri, cnt = pl.pallas_call(
        functools.partial(_oproj_router_kernel, combine=combine),
        grid=(nt,),
        in_specs=[row(d)] + attn_specs + [full((hd, d)), full((1, d)), full((d, LANES)), full((1, LANES)),
                                          full((tm, tm))],
        out_specs=[row(d), row(d), row(LANES),
                   pl.BlockSpec((None, SUBLANES, tm), lambda i: (i, 0, 0)),
                   full((SUBLANES, LANES))],
        out_shape=[jax.ShapeDtypeStruct((t, d), F32), jax.ShapeDtypeStruct((t, d), F32),
                   jax.ShapeDtypeStruct((t, LANES), F32),
                   jax.ShapeDtypeStruct((nt, SUBLANES, tm), I32),
                   jax.ShapeDtypeStruct((SUBLANES, LANES), F32)],
        scratch_shapes=[pltpu.VMEM((SUBLANES, LANES), F32)],
        compiler_params=_params(("arbitrary",)),
        name="oproj_router_b" if combine else "oproj_router_a",
    )(x, *attn_args, wo_bf, g_ffn.reshape(1, d), wr, br, ltri)
    return y, h, rw, ri, cnt[0, :N_EXPERTS].astype(I32)


def _dispatch_kernel(offs_ref, ri_ref, h_ref, hs_ref, sem):
    tm = ri_ref.shape[-1]

    def issue(r, carry):
        src = h_ref.at[pl.ds(r, 1)]
        s1 = offs_ref[ri_ref[0, r]] + ri_ref[2, r]
        s2 = offs_ref[ri_ref[1, r]] + ri_ref[3, r]
        pltpu.make_async_copy(src, hs_ref.at[pl.ds(s1, 1)], sem).start()
        pltpu.make_async_copy(src, hs_ref.at[pl.ds(s2, 1)], sem).start()
        return carry

    lax.fori_loop(0, tm, issue, 0)

    def drain(r, carry):
        pltpu.make_async_copy(h_ref.at[pl.ds(0, 1)], hs_ref.at[pl.ds(0, 1)], sem).wait()
        pltpu.make_async_copy(h_ref.at[pl.ds(0, 1)], hs_ref.at[pl.ds(0, 1)], sem).wait()
        return carry

    lax.fori_loop(0, tm, drain, 0)


def _dispatch(h, ri, offs, n_rows):
    t, d = h.shape
    nt, _, tm = ri.shape
    grid_spec = pltpu.PrefetchScalarGridSpec(
        num_scalar_prefetch=1,
        grid=(nt,),
        in_specs=[pl.BlockSpec((None, SUBLANES, tm), lambda i, offs: (i, 0, 0), memory_space=pltpu.SMEM),
                  pl.BlockSpec((tm, d), lambda i, offs: (i, 0))],
        out_specs=pl.BlockSpec(memory_space=pl.ANY),
        scratch_shapes=[pltpu.SemaphoreType.DMA(())])
    return pl.pallas_call(
        _dispatch_kernel,
        grid_spec=grid_spec,
        out_shape=jax.ShapeDtypeStruct((n_rows, d), F32),
        compiler_params=pltpu.CompilerParams(dimension_semantics=("arbitrary",), has_side_effects=True),
        name="moe_dispatch",
    )(offs, ri, h)


def _experts_kernel(blk_ref, exp_ref, used_ref, x_ref, wg_ref, wu_ref, wd_ref, o_ref):
    i = pl.program_id(0)

    @pl.when(i < used_ref[0])
    def _():
        x = x_ref[...].astype(BF16)
        a = _dot(x, wg_ref[...])
        u = _dot(x, wu_ref[...])
        mid = a * (1.0 / (1.0 + jnp.exp(-a))) * u
        o_ref[...] = _dot(mid.astype(BF16), wd_ref[...])


def _experts(hs, wg_bf, wu_bf, wd_bf, tile_blk, tile_exp, n_used, tm):
    n_rows, d = hs.shape
    f = wg_bf.shape[-1]
    n_tiles = n_rows // tm
    grid_spec = pltpu.PrefetchScalarGridSpec(
        num_scalar_prefetch=3,
        grid=(n_tiles,),
        in_specs=[pl.BlockSpec((tm, d), lambda i, blk, ex, used: (blk[i], 0)),
                  pl.BlockSpec((None, d, f), lambda i, blk, ex, used: (ex[i], 0, 0)),
                  pl.BlockSpec((None, d, f), lambda i, blk, ex, used: (ex[i], 0, 0)),
                  pl.BlockSpec((None, f, d), lambda i, blk, ex, used: (ex[i], 0, 0))],
        out_specs=pl.BlockSpec((tm, d), lambda i, blk, ex, used: (blk[i], 0)))
    return pl.pallas_call(
        _experts_kernel,
        grid_spec=grid_spec,
        out_shape=jax.ShapeDtypeStruct((n_rows, d), F32),
        compiler_params=_params(("arbitrary",)),
        name="moe_experts",
    )(tile_blk, tile_exp, n_used, hs, wg_bf, wu_bf, wd_bf)


def _combine_kernel(offs_ref, ri_ref, y_ref, rw_ref, ys_ref, o_ref, g1, g2, sem):
    tm = y_ref.shape[0]

    def issue(r, carry):
        s1 = offs_ref[ri_ref[0, r]] + ri_ref[2, r]
        s2 = offs_ref[ri_ref[1, r]] + ri_ref[3, r]
        pltpu.make_async_copy(ys_ref.at[pl.ds(s1, 1)], g1.at[pl.ds(r, 1)], sem).start()
        pltpu.make_async_copy(ys_ref.at[pl.ds(s2, 1)], g2.at[pl.ds(r, 1)], sem).start()
        return carry

    lax.fori_loop(0, tm, issue, 0)

    def drain(r, carry):
        pltpu.make_async_copy(ys_ref.at[pl.ds(0, 1)], g1.at[pl.ds(0, 1)], sem).wait()
        pltpu.make_async_copy(ys_ref.at[pl.ds(0, 1)], g2.at[pl.ds(0, 1)], sem).wait()
        return carry

    lax.fori_loop(0, tm, drain, 0)
    rw = rw_ref[...]
    o_ref[...] = y_ref[...] + rw[:, 0:1] * g1[...] + rw[:, 1:2] * g2[...]


def _combine(y, rw, ri, offs, ys):
    t, d = y.shape
    nt, _, tm = ri.shape
    grid_spec = pltpu.PrefetchScalarGridSpec(
        num_scalar_prefetch=1,
        grid=(nt,),
        in_specs=[pl.BlockSpec((None, SUBLANES, tm), lambda i, offs: (i, 0, 0), memory_space=pltpu.SMEM),
                  pl.BlockSpec((tm, d), lambda i, offs: (i, 0)),
                  pl.BlockSpec((tm, LANES), lambda i, offs: (i, 0)),
                  pl.BlockSpec(memory_space=pl.ANY)],
        out_specs=pl.BlockSpec((tm, d), lambda i, offs: (i, 0)),
        scratch_shapes=[pltpu.VMEM((tm, d), F32), pltpu.VMEM((tm, d), F32), pltpu.SemaphoreType.DMA(())])
    return pl.pallas_call(
        _combine_kernel,
        grid_spec=grid_spec,
        out_shape=jax.ShapeDtypeStruct((t, d), F32),
        compiler_params=_params(("arbitrary",)),
        name="moe_combine",
    )(offs, ri, y, rw, ys)


def _moe(y, h, rw, ri, counts, wg_bf, wu_bf, wd_bf, tm_e):
    t = y.shape[0]
    n_tiles = 2 * t // tm_e + N_EXPERTS
    tiles_per = (counts + tm_e - 1) // tm_e
    tile_end = jnp.cumsum(tiles_per)
    offs = ((tile_end - tiles_per) * tm_e).astype(I32)
    n_used = tile_end[-1].astype(I32)
    tile_id = jnp.minimum(jnp.arange(n_tiles, dtype=I32), n_used - 1)
    tile_exp = jnp.sum(tile_id[:, None] >= tile_end[None, :], axis=1).astype(I32)
    hs = _dispatch(h, ri, offs, n_tiles * tm_e)
    ys = _experts(hs, wg_bf, wu_bf, wd_bf, tile_id, tile_exp, n_used.reshape(1), tm_e)
    return _combine(y, rw, ri, offs, ys)


def kernel(x_prompt, x_sample, cache_kv_a, cache_kv_b1, cache_kv_b2, cache_kv_b3, page_table, g_attn, g_ffn, w_qkv_a, w_o_a, sb_bias, g_kv, w_kv_b, k_norm_b, w_q_b, q_norm_b, w_o_b, w_router_group, b_router_group, w_router_expert, b_router_expert, w_gate, w_up, w_down):
    b, s, d = x_prompt.shape
    db, t, _ = x_sample.shape
    assert t == 1
    n_a, n_pool, page, _, n_heads_a, _ = cache_kv_a.shape
    depth = g_attn.shape[0]
    n_kv = cache_kv_b1.shape[3]
    n_heads_b = w_q_b.shape[-1] // (N_DIL * HEAD_DIM)
    hd_a = n_heads_a * HEAD_DIM
    past_len = page_table.shape[1] * page
    buffers = (cache_kv_b1, cache_kv_b2, cache_kv_b3)
    tm_p, tm_s = 256, db * t
    tme_p, tme_s = 256, 32

    xp = x_prompt.reshape(b * s, d)
    xs = x_sample.reshape(db * t, d)
    wg_bf, wu_bf, wd_bf = w_gate.astype(BF16), w_up.astype(BF16), w_down.astype(BF16)
    new_a_p, new_a_s = [], []
    kv_p = kv_s = None
    for layer in range(depth):
        if layer < n_a:
            i = layer
            wqkv = w_qkv_a[i].astype(BF16)
            wo = w_o_a[i].astype(BF16)
            qkv_p = _proj_a(xp, g_attn[layer], wqkv, tm_p)
            qkv_s = _proj_a(xs, g_attn[layer], wqkv, tm_s)
            o_p = _sb_prompt(qkv_p.reshape(b, s, 3 * hd_a), sb_bias[i], n_heads_a)
            o_s = _sb_sample(qkv_s[:, :hd_a].reshape(db, n_heads_a, HEAD_DIM), cache_kv_a[i], page_table,
                             sb_bias[i])
            attn_p = (o_p.reshape(b * s, hd_a),)
            attn_s = (o_s.reshape(db, hd_a),)
            new_a_p.append(qkv_p[:, hd_a:].reshape(b, s, 2, n_heads_a, HEAD_DIM))
            new_a_s.append(qkv_s[:, hd_a:].reshape(db, t, 2, n_heads_a, HEAD_DIM))
        else:
            j = layer - n_a
            wo = w_o_b[j].astype(BF16)
            wq = w_q_b[j].astype(BF16)
            wkv = w_kv_b.astype(BF16)
            tab_p = _rope_tables(jnp.arange(s))
            tab_s = _rope_tables(jnp.full((tm_s,), past_len))
            kv_pj, q_p = _proj_b(xp, g_kv, wkv, g_attn[layer], wq, k_norm_b, q_norm_b[j], tab_p, tm_p,
                                 n_heads_b, n_kv)
            kv_sj, q_s = _proj_b(xs, g_kv, wkv, g_attn[layer], wq, k_norm_b, q_norm_b[j], tab_s, tm_s,
                                 n_heads_b, n_kv)
            if j == 0:
                kv_p, kv_s = kv_pj, kv_sj
            outs_p, lses_p, outs_s, lses_s = [], [], [], []
            for g in range(N_DIL):
                o, l = _dil_prompt(q_p.reshape(b, s, -1), kv_p.reshape(b, s, -1), g, n_heads_b, n_kv)
                outs_p.append(o)
                lses_p.append(l)
                o, l = _dil_sample(q_s.reshape(db, 1, -1), buffers[g].reshape(db, buffers[g].shape[1], -1),
                                   kv_s.reshape(db, 1, -1), g, n_heads_b, n_kv)
                outs_s.append(o)
                lses_s.append(l)
            attn_p = tuple(outs_p) + tuple(lses_p)
            attn_s = tuple(outs_s) + tuple(lses_s)
        router = (g_ffn[layer], w_router_group[layer], b_router_group[layer],
                  w_router_expert[layer], b_router_expert[layer])
        experts = (wg_bf[layer], wu_bf[layer], wd_bf[layer])
        xp = _moe(*_oproj_router(xp, attn_p, wo, *router, tm_p), *experts, tme_p)
        xs = _moe(*_oproj_router(xs, attn_s, wo, *router, tm_s), *experts, tme_s)

    kv_p5 = kv_p.reshape(b, s, N_DIL, 2, n_kv, HEAD_DIM)
    kv_s5 = kv_s.reshape(db, t, N_DIL, 2, n_kv, HEAD_DIM)
    outs = [xp.reshape(b, s, d), xs.reshape(db, t, d), jnp.stack(new_a_p), jnp.stack(new_a_s)]
    for g, (window, _) in enumerate(DILATED):
        outs.append(kv_p5[:, s - min(window, s):, g])
        outs.append(kv_s5[:, :, g])
    return tuple(outs)
```

```python
import functools

import numpy as np
import jax
import jax.numpy as jnp
from jax import lax
from jax.experimental import pallas as pl
from jax.experimental.pallas import tpu as pltpu

F32 = jnp.float32
BF16 = jnp.bfloat16
I32 = jnp.int32

HEAD_DIM = 64
ROT_DIM = HEAD_DIM // 4
ROPE_THETA = 500000.0
DILATED = ((128, 1), (512, 4), (2048, 16))
N_DIL = len(DILATED)
BLOCK = 128
N_GROUPS = 4
EXPERTS_PER_GROUP = 8
N_EXPERTS = N_GROUPS * EXPERTS_PER_GROUP
RMS_EPS = 1e-6
NEG_INF = -1e30

LANES = 128
SUBLANES = 8
VMEM_LIMIT = 48 * 1024 * 1024
GROUP_LANE0 = N_EXPERTS


def _dot(a, b):
    return jnp.dot(a, b, preferred_element_type=F32)


def _dot_nt(a, b):
    return lax.dot_general(a, b, (((1,), (1,)), ((), ())), preferred_element_type=F32)


def _split2(x):
    hi = x.astype(BF16)
    lo = (x - hi.astype(F32)).astype(BF16)
    return hi, lo


def _split3(x):
    hi = x.astype(BF16)
    r = x - hi.astype(F32)
    mid = r.astype(BF16)
    lo = (r - mid.astype(F32)).astype(BF16)
    return hi, mid, lo


def _rms_scale(x):
    return x * lax.rsqrt(jnp.mean(x * x, axis=-1, keepdims=True) + RMS_EPS)


def _params(sem):
    return pltpu.CompilerParams(dimension_semantics=sem, vmem_limit_bytes=VMEM_LIMIT)


def _proj_a_kernel(x_ref, g_ref, w_ref, o_ref, *maybe_kvt_ref, tn):
    xn = (_rms_scale(x_ref[...]) * g_ref[...]).astype(BF16)
    n = o_ref.shape[-1]
    for c in range(n // tn):
        y = _dot(xn, w_ref[:, c * tn:(c + 1) * tn])
        o_ref[:, c * tn:(c + 1) * tn] = y
        if maybe_kvt_ref and c * tn >= n // 3:
            row = c * tn - n // 3
            maybe_kvt_ref[0][row:row + tn, :] = y.T


def _proj_a(x, g, w_bf, tm, seq=None):
    t, d = x.shape
    n = w_bf.shape[1]
    out_specs = [pl.BlockSpec((tm, n), lambda i: (i, 0))]
    out_shape = [jax.ShapeDtypeStruct((t, n), F32)]
    if seq is not None:
        per = seq // tm
        out_specs.append(pl.BlockSpec((None, 2 * n // 3, tm), lambda i: (i // per, 0, i % per)))
        out_shape.append(jax.ShapeDtypeStruct((t // seq, 2 * n // 3, seq), F32))
    return pl.pallas_call(
        functools.partial(_proj_a_kernel, tn=512),
        grid=(t // tm,),
        in_specs=[pl.BlockSpec((tm, d), lambda i: (i, 0)),
                  pl.BlockSpec((1, d), lambda i: (0, 0)),
                  pl.BlockSpec((d, n), lambda i: (0, 0))],
        out_specs=out_specs,
        out_shape=out_shape,
        compiler_params=_params(("parallel",)),
        name="proj_a",
    )(x, g.reshape(1, d), w_bf)


def _headnorm_rope(y, gain, bd, c, s1, s2):
    hi, lo = _split2(y * y)
    ms = (_dot(hi, bd) + _dot(lo, bd)) * (1.0 / HEAD_DIM)
    yn = y * lax.rsqrt(ms + RMS_EPS) * gain
    half = ROT_DIM // 2
    return yn * c + pltpu.roll(yn, LANES - half, 1) * s1 + pltpu.roll(yn, half, 1) * s2


def _proj_b_kernel(x_ref, gkv_ref, wkv_ref, gq_ref, wq_ref, gainkv_ref, gainq_ref, bd_ref,
                   c_ref, s1_ref, s2_ref, kv_ref, q_ref, *, tn):
    xh = _rms_scale(x_ref[...])
    bd = bd_ref[...]
    c, s1, s2 = c_ref[...], s1_ref[...], s2_ref[...]
    kv_cols = 2 * (kv_ref.shape[-1] // (2 * N_DIL))

    def run(g_ref, w_ref, gain_ref, o_ref, is_key):
        xn = (xh * g_ref[...]).astype(BF16)
        for cc in range(o_ref.shape[-1] // tn):
            y = _dot(xn, w_ref[:, cc * tn:(cc + 1) * tn])
            for s in range(tn // LANES):
                col = cc * tn + s * LANES
                ys = y[:, s * LANES:(s + 1) * LANES]
                if is_key(col):
                    ys = _headnorm_rope(ys, gain_ref[:, col:col + LANES], bd, c, s1, s2)
                o_ref[:, col:col + LANES] = ys

    run(gkv_ref, wkv_ref, gainkv_ref, kv_ref, lambda col: (col % kv_cols) < kv_cols // 2)
    run(gq_ref, wq_ref, gainq_ref, q_ref, lambda col: True)


def _rope_tables(pos):
    half = ROT_DIM // 2
    inv = jnp.power(ROPE_THETA, -jnp.arange(half, dtype=F32) * 2.0 / ROT_DIM)
    ang = pos.astype(F32)[:, None] * inv[None, :]
    cos, sin = jnp.cos(ang), jnp.sin(ang)
    n = pos.shape[0]
    ones = jnp.ones((n, HEAD_DIM - ROT_DIM), F32)
    zeros = jnp.zeros((n, HEAD_DIM - ROT_DIM), F32)
    zh = jnp.zeros((n, half), F32)
    c = jnp.concatenate([cos, cos, ones], axis=1)
    s1 = jnp.concatenate([-sin, zh, zeros], axis=1)
    s2 = jnp.concatenate([zh, sin, zeros], axis=1)
    rep = LANES // HEAD_DIM
    return tuple(jnp.tile(a, (1, rep)) for a in (c, s1, s2))


def _proj_b(x, g_kv, wkv_bf, g_q, wq_bf, k_norm, q_norm, tables, tm, n_heads_q, n_kv):
    t, d = x.shape
    nkv, nq = wkv_bf.shape[1], wq_bf.shape[1]
    gain_kv = jnp.broadcast_to(k_norm[:, None, None, :], (N_DIL, 2, n_kv, HEAD_DIM)).reshape(1, nkv)
    gain_q = jnp.broadcast_to(q_norm[:, None, :], (N_DIL, n_heads_q, HEAD_DIM)).reshape(1, nq)
    lane = np.arange(LANES)
    bd = jnp.asarray((lane[:, None] // HEAD_DIM == lane[None, :] // HEAD_DIM), BF16)
    n_tab = tables[0].shape[0] // tm
    tab_spec = pl.BlockSpec((tm, LANES), lambda i: (i % n_tab, 0))
    full = lambda shape: pl.BlockSpec(shape, lambda i: (0, 0))
    return pl.pallas_call(
        functools.partial(_proj_b_kernel, tn=512),
        grid=(t // tm,),
        in_specs=[pl.BlockSpec((tm, d), lambda i: (i, 0)),
                  full((1, d)), full((d, nkv)), full((1, d)), full((d, nq)),
                  full((1, nkv)), full((1, nq)), full((LANES, LANES)),
                  tab_spec, tab_spec, tab_spec],
        out_specs=[pl.BlockSpec((tm, nkv), lambda i: (i, 0)),
                   pl.BlockSpec((tm, nq), lambda i: (i, 0))],
        out_shape=[jax.ShapeDtypeStruct((t, nkv), F32), jax.ShapeDtypeStruct((t, nq), F32)],
        compiler_params=_params(("parallel",)),
        name="proj_b",
    )(x, g_kv.reshape(1, d), wkv_bf, g_q.reshape(1, d), wq_bf, gain_kv, gain_q, bd, *tables)


def _tree_prod(xs):
    while len(xs) > 1:
        xs = [xs[i] * xs[i + 1] for i in range(0, len(xs) - 1, 2)] + ([xs[-1]] if len(xs) % 2 else [])
    return xs[0]


def _sb_prompt_kernel(bias_ref, pm_ref, q_ref, k_ref, v_ref, o_ref, kbf, vtb, acc, zbuf, abuf, *, blk,
                      scan_blocks):
    hp = pl.program_id(1)
    qi = pl.program_id(2)
    seg = blk // SUBLANES
    per = seg // scan_blocks
    wide = 2 * blk
    n_blocks = kbf.shape[0]

    @pl.when(qi == 0)
    def _():
        pm = pm_ref[...]

        def prep(j, carry):
            rows = pl.ds(pl.multiple_of(j * blk, blk), blk)
            kbf[j] = _dot(pm, k_ref[rows, :].astype(BF16)).astype(BF16)
            vtb[j] = _dot(pm, v_ref[rows, :].astype(BF16)).T.astype(BF16)
            return carry

        lax.fori_loop(0, n_blocks, prep, 0)

    q = q_ref[...] * (HEAD_DIM ** -0.5)
    lane = lax.broadcasted_iota(I32, q.shape, 1)
    q2 = jnp.concatenate([jnp.where(lane < HEAD_DIM, q, 0.0), jnp.where(lane >= HEAD_DIM, q, 0.0)],
                         axis=0).astype(BF16)
    lane_w = lax.broadcasted_iota(I32, (1, wide), 1)
    bias = jnp.where(lane_w < blk, bias_ref[2 * hp], bias_ref[2 * hp + 1])

    sub = lax.broadcasted_iota(I32, (SUBLANES, wide), 0)

    def scores(j):
        return _dot_nt(kbf[j], q2) + bias

    def gates(z, masked):
        e = jnp.exp(-jnp.abs(z))
        r = 1.0 / (1.0 + e)
        s = e * r
        pos = z >= 0.0
        beta = jnp.where(pos, r, s)
        om = jnp.where(pos, s, r)
        if masked:
            rho = lax.broadcasted_iota(I32, (blk, wide), 0)
            key_idx = (rho & (SUBLANES - 1)) * seg + (rho >> 3)
            query_idx = lax.broadcasted_iota(I32, (blk, wide), 1) & (blk - 1)
            diag = key_idx < query_idx
            beta = jnp.where(diag, beta, 0.0)
            om = jnp.where(diag, om, 1.0)
        return beta, om

    def scan(beta, om, carry):
        oms = [om[v * SUBLANES:(v + 1) * SUBLANES, :] for v in range(seg)]
        chain_tot = [_tree_prod(oms[c * per:(c + 1) * per]) for c in range(scan_blocks)]
        inc = _tree_prod(chain_tot)
        for d in (1, 2, 4):
            inc = inc * jnp.where(sub + d < SUBLANES, pltpu.roll(inc, SUBLANES - d, 0), 1.0)
        exc = jnp.where(sub + 1 < SUBLANES, pltpu.roll(inc, SUBLANES - 1, 0), 1.0)
        new_carry = inc[0:1, :] * carry
        start = [None] * scan_blocks
        start[scan_blocks - 1] = exc * carry
        for c in range(scan_blocks - 2, -1, -1):
            start[c] = start[c + 1] * chain_tot[c + 1]
        rows = [None] * seg
        for c in range(scan_blocks):
            run = start[c]
            for v in range((c + 1) * per - 1, c * per - 1, -1):
                rows[v] = beta[v * SUBLANES:(v + 1) * SUBLANES, :] * run
                if v > c * per:
                    run = run * oms[v]
        return jnp.concatenate(rows, axis=0).astype(BF16), new_carry

    ones = jnp.ones((SUBLANES, wide), F32)
    abuf[...], carry = scan(*gates(scores(qi), True), ones)
    zbuf[...] = scores(jnp.maximum(qi - 1, 0))
    acc[...] = jnp.zeros_like(acc)

    def body(t, carry):
        j = qi - 1 - t
        acc[...] += _dot(vtb[j + 1], abuf[...])
        beta, om = gates(zbuf[...], False)
        zbuf[...] = scores(jnp.maximum(j - 1, 0))
        abuf[...], carry = scan(beta, om, carry)
        return carry

    lax.fori_loop(0, qi, body, carry)
    acc[...] += _dot(vtb[0], abuf[...])
    row = lax.broadcasted_iota(I32, (LANES, blk), 0)
    o_ref[...] = jnp.where(row < HEAD_DIM, acc[:, :blk], acc[:, blk:]).T


def _sb_prompt(qkv, bias, n_heads, blk=256):
    b, s, _ = qkv.shape
    n_hp = n_heads * HEAD_DIM // LANES
    n_blocks = s // blk
    seg = blk // SUBLANES
    rho = np.arange(blk)
    pm = jnp.asarray(((rho % SUBLANES) * seg + rho // SUBLANES)[:, None] == np.arange(blk)[None, :], BF16)
    return pl.pallas_call(
        functools.partial(_sb_prompt_kernel, blk=blk, scan_blocks=4),
        grid=(b, n_hp, n_blocks),
        in_specs=[pl.BlockSpec(memory_space=pltpu.SMEM),
                  pl.BlockSpec((blk, blk), lambda bi, hp, qi: (0, 0)),
                  pl.BlockSpec((None, blk, LANES), lambda bi, hp, qi: (bi, qi, hp)),
                  pl.BlockSpec((None, s, LANES), lambda bi, hp, qi: (bi, 0, n_hp + hp)),
                  pl.BlockSpec((None, s, LANES), lambda bi, hp, qi: (bi, 0, 2 * n_hp + hp))],
        out_specs=pl.BlockSpec((None, blk, LANES), lambda bi, hp, qi: (bi, qi, hp)),
        out_shape=jax.ShapeDtypeStruct((b, s, n_heads * HEAD_DIM), F32),
        scratch_shapes=[pltpu.VMEM((n_blocks, blk, LANES), BF16),
                        pltpu.VMEM((n_blocks, LANES, blk), BF16),
                        pltpu.VMEM((LANES, 2 * blk), F32),
                        pltpu.VMEM((blk, 2 * blk), F32),
                        pltpu.VMEM((blk, 2 * blk), BF16)],
        compiler_params=_params(("parallel", "parallel", "arbitrary")),
        name="sb_prompt",
    )(bias, pm, qkv, qkv, qkv)


def _softplus(x):
    return jnp.maximum(x, 0.0) + jnp.log1p(jnp.exp(-jnp.abs(x)))


def _dot3(x, w):
    return sum(_dot(part, w) for part in _split3(x))


def _sb_sample_kernel(pt_ref, q_ref, *refs, nseq):
    kv_refs = refs[:nseq]
    bias_ref, hmask_ref, ut_ref, o_ref, carry, acc = refs[nseq:]
    p = pl.program_id(1)
    _, n_heads, _, page = kv_refs[0].shape
    hd = n_heads * HEAD_DIM

    @pl.when(p == 0)
    def _():
        carry[...] = jnp.zeros_like(carry)
        acc[...] = jnp.zeros_like(acc)

    hmask = hmask_ref[...]
    z_rows, values = [], []
    for s in range(nseq):
        q_rows = (hmask * (q_ref[s] * (HEAD_DIM ** -0.5))).astype(BF16)
        k_t = kv_refs[s][0].reshape(hd, page).astype(BF16)
        values.append(kv_refs[s][1].reshape(hd, page).astype(BF16))
        z_rows.append(_dot(q_rows, k_t))
    z = jnp.concatenate(z_rows, axis=0) + bias_ref[...]
    log_rest = -_softplus(z)
    log_beta = log_rest + z
    tail = _dot3(log_rest, ut_ref[...]) + carry[...]
    a = jnp.exp(log_beta + tail).astype(BF16)
    carry[...] += jnp.sum(log_rest, axis=-1, keepdims=True)
    for s in range(nseq):
        acc[s] += _dot_nt(a[s * n_heads:(s + 1) * n_heads, :], values[s])

    @pl.when(p == pl.num_programs(1) - 1)
    def _():
        for s in range(nseq):
            o_ref[s] = jnp.sum(acc[s] * hmask, axis=0, keepdims=True)


def _sb_sample(q3, cache_t, layer, page_table, bias, nseq=4):
    db, n_pages = page_table.shape
    _, _, _, n_heads, _, page = cache_t.shape
    hd = n_heads * HEAD_DIM
    hmask = jnp.asarray(np.arange(hd)[None, :] // HEAD_DIM == np.arange(n_heads)[:, None], F32)
    ut = jnp.asarray(np.arange(page)[:, None] > np.arange(page)[None, :], BF16)
    bias2 = jnp.tile(jnp.broadcast_to(bias[:, None], (n_heads, page)), (nseq, 1))
    const = lambda shape: pl.BlockSpec(shape, lambda b, p, pt: (0, 0))
    page_spec = lambda s: pl.BlockSpec(
        (None, None, 2, n_heads, HEAD_DIM, page),
        lambda b, p, pt: (layer, pt[b * nseq + s, n_pages - 1 - p], 0, 0, 0, 0))
    grid_spec = pltpu.PrefetchScalarGridSpec(
        num_scalar_prefetch=1,
        grid=(db // nseq, n_pages),
        in_specs=[pl.BlockSpec((nseq, 1, hd), lambda b, p, pt: (b, 0, 0))]
                 + [page_spec(s) for s in range(nseq)]
                 + [const((nseq * n_heads, page)), const((n_heads, hd)), const((page, page))],
        out_specs=pl.BlockSpec((nseq, 1, hd), lambda b, p, pt: (b, 0, 0)),
        scratch_shapes=[pltpu.VMEM((nseq * n_heads, page), F32), pltpu.VMEM((nseq, n_heads, hd), F32)])
    return pl.pallas_call(
        functools.partial(_sb_sample_kernel, nseq=nseq),
        grid_spec=grid_spec,
        out_shape=jax.ShapeDtypeStruct((db, 1, hd), F32),
        compiler_params=_params(("parallel", "arbitrary")),
        name="sb_sample",
    )(page_table, q3, *([cache_t] * nseq), bias2, hmask, ut)


def _dil_prompt_kernel(q_ref, kp_ref, kc_ref, vp_ref, vc_ref, rep_ref, o_ref, lse_ref, *, steps, n_kv):
    i = pl.program_id(2)
    blk = q_ref.shape[0]
    width = kc_ref.shape[-1]
    per_kv = q_ref.shape[-1] // width
    k2 = jnp.concatenate([kp_ref[...], kc_ref[...]], axis=0).astype(BF16)
    v2 = jnp.concatenate([vp_ref[...], vc_ref[...]], axis=0).astype(BF16)
    n_heads = n_kv * per_kv
    row = lax.broadcasted_iota(I32, (n_heads * blk, 2 * blk), 0) & (blk - 1)
    col = lax.broadcasted_iota(I32, (n_heads * blk, 2 * blk), 1)
    rel = blk + row - col
    first_col = jnp.where(i > 0, 0, blk)
    mask = (rel >= 0) & (rel <= steps) & (col >= first_col)
    lane = lax.broadcasted_iota(I32, (blk, width), 1)
    sels = [(lane >= j * HEAD_DIM) & (lane < (j + 1) * HEAD_DIM) for j in range(per_kv)]
    scale = HEAD_DIM ** -0.5
    s_parts, v_reps = [], []
    for c in range(n_kv):
        rep = rep_ref[c]
        k_rep = _dot(k2, rep).astype(BF16)
        v_reps.append(_dot(v2, rep).astype(BF16))
        qc = q_ref[:, c * width:(c + 1) * width] * scale
        qs = jnp.concatenate([jnp.where(sel, qc, 0.0) for sel in sels], axis=0).astype(BF16)
        s_parts.append(_dot_nt(qs, k_rep))
    s = jnp.where(mask, jnp.concatenate(s_parts, axis=0), NEG_INF)
    m = jnp.max(s, axis=-1, keepdims=True)
    pexp = jnp.exp(s - m)
    den = jnp.sum(pexp, axis=-1, keepdims=True)
    lse = m + jnp.log(den)
    p = (pexp * (1.0 / den)).astype(BF16)
    for c in range(n_kv):
        oc = _dot(p[c * per_kv * blk:(c + 1) * per_kv * blk, :], v_reps[c])
        o_c = jnp.zeros((blk, width), F32)
        for j, sel in enumerate(sels):
            o_c = o_c + jnp.where(sel, oc[j * blk:(j + 1) * blk, :], 0.0)
        o_ref[:, c * width:(c + 1) * width] = o_c
    lane_l = lax.broadcasted_iota(I32, (blk, LANES), 1)
    lse_tile = jnp.zeros((blk, LANES), F32)
    for h in range(n_heads):
        lse_tile = lse_tile + jnp.where(lane_l == h, lse[h * blk:(h + 1) * blk, :], 0.0)
    lse_ref[...] = lse_tile


def _rep_matrices(n_kv):
    width = n_kv * HEAD_DIM
    src = np.arange(width)[:, None]
    dst = np.arange(width)[None, :]
    return jnp.asarray(np.stack([(src == c * HEAD_DIM + dst % HEAD_DIM) for c in range(n_kv)]), BF16)


def _dil_prompt(q, kv, g, n_heads, n_kv):
    b, s, nq = q.shape
    nkv = kv.shape[-1]
    window, dil = DILATED[g]
    steps = window // dil
    hd = n_heads * HEAD_DIM
    width = n_kv * HEAD_DIM
    n_sub = s // dil
    nb = n_sub // BLOCK
    qv = q.reshape(b, n_sub, dil * nq)
    kvv = kv.reshape(b, n_sub, dil * nkv)
    q_blocks = nq // hd
    kv_blocks = nkv // width
    kcol = lambda r: r * kv_blocks + g * 2
    out = pl.pallas_call(
        functools.partial(_dil_prompt_kernel, steps=steps, n_kv=n_kv),
        grid=(b, dil, nb),
        in_specs=[pl.BlockSpec((None, BLOCK, hd), lambda bi, r, i: (bi, i, r * q_blocks + g)),
                  pl.BlockSpec((None, BLOCK, width), lambda bi, r, i: (bi, jnp.maximum(i - 1, 0), kcol(r))),
                  pl.BlockSpec((None, BLOCK, width), lambda bi, r, i: (bi, i, kcol(r))),
                  pl.BlockSpec((None, BLOCK, width), lambda bi, r, i: (bi, jnp.maximum(i - 1, 0), kcol(r) + 1)),
                  pl.BlockSpec((None, BLOCK, width), lambda bi, r, i: (bi, i, kcol(r) + 1)),
                  pl.BlockSpec((n_kv, width, width), lambda bi, r, i: (0, 0, 0))],
        out_specs=[pl.BlockSpec((None, BLOCK, hd), lambda bi, r, i: (bi, i, r)),
                   pl.BlockSpec((None, BLOCK, LANES), lambda bi, r, i: (bi, i, r))],
        out_shape=[jax.ShapeDtypeStruct((b, n_sub, dil * hd), F32),
                   jax.ShapeDtypeStruct((b, n_sub, dil * LANES), F32)],
        compiler_params=_params(("parallel", "parallel", "parallel")),
        name=f"dil_prompt_{g}",
    )(qv, kvv, kvv, kvv, kvv, _rep_matrices(n_kv))
    return out[0].reshape(b * s, hd), out[1].reshape(b * s, LANES)


def _dil_sample_kernel(q_ref, buf_ref, kvn_ref, pick_ref, fold_ref, unfold_ref, hmask_ref, o_ref, lse_ref, *,
                       nseq):
    width = kvn_ref.shape[-1] // 2
    n_heads, hd = hmask_ref.shape
    lb, n_keys = pick_ref.shape
    hmask = hmask_ref[...]
    fold = fold_ref[...]
    unfold = unfold_ref[...]
    pick = pick_ref[...]
    scale = HEAD_DIM ** -0.5
    key_ok = jnp.sum(pick.astype(F32), axis=0, keepdims=True) > 0.0
    eye = (lax.broadcasted_iota(I32, (n_heads, LANES), 0) == lax.broadcasted_iota(I32, (n_heads, LANES), 1))
    for n in range(nseq):
        q_rows = (hmask * (q_ref[n] * scale)).astype(BF16)
        qm = _dot(q_rows, fold)
        k_t = _dot(buf_ref[n, 0].reshape(width, lb).astype(BF16), pick).astype(BF16)
        v_t = _dot(buf_ref[n, 1].reshape(width, lb).astype(BF16), pick).astype(BF16)
        k_new = kvn_ref[n, :, :width]
        v_new = kvn_ref[n, :, width:]
        s = jnp.where(key_ok, _dot(qm.astype(BF16), k_t), NEG_INF)
        s_new = jnp.sum(qm * k_new, axis=-1, keepdims=True)
        m = jnp.maximum(jnp.max(s, axis=-1, keepdims=True), s_new)
        den = jnp.sum(jnp.exp(s - m), axis=-1, keepdims=True) + jnp.exp(s_new - m)
        lse = m + jnp.log(den)
        p = jnp.exp(s - lse)
        p_new = jnp.exp(s_new - lse)
        o = _dot_nt(p.astype(BF16), v_t) + p_new * v_new
        o_full = _dot(o.astype(BF16), unfold) * hmask
        o_ref[n] = jnp.sum(o_full, axis=0, keepdims=True)
        lse_ref[n] = jnp.sum(jnp.where(eye, lse, 0.0), axis=0, keepdims=True)


def _dil_sample(q3, buf, kv3, g, n_heads, n_kv):
    db, lb = buf.shape[:2]
    window, dil = DILATED[g]
    steps = window // dil
    hd = n_heads * HEAD_DIM
    width = n_kv * HEAD_DIM
    n_keys = min(steps, lb // dil)
    rows_read = lb - (n_keys - np.arange(n_keys)) * dil
    pick = jnp.asarray(np.arange(lb)[:, None] == rows_read[None, :], BF16)
    buf_t = buf.transpose(0, 2, 3, 4, 1)
    nseq = max(1, min(8, (8 * 1024 * 1024) // (2 * width * lb * 4)))
    per_kv = n_heads // n_kv
    src = np.arange(hd)[:, None]
    dst = np.arange(width)[None, :]
    fold_np = (src // (HEAD_DIM * per_kv) == dst // HEAD_DIM) & (src % HEAD_DIM == dst % HEAD_DIM)
    fold = jnp.asarray(fold_np, BF16)
    unfold = jnp.asarray(fold_np.T, BF16)
    hmask = jnp.asarray(np.arange(hd)[None, :] // HEAD_DIM == np.arange(n_heads)[:, None], F32)
    full2 = lambda shape: pl.BlockSpec(shape, lambda i: (0, 0))
    o, lse = pl.pallas_call(
        functools.partial(_dil_sample_kernel, nseq=nseq),
        grid=(db // nseq,),
        in_specs=[pl.BlockSpec((nseq, 1, hd), lambda i: (i, 0, g)),
                  pl.BlockSpec((nseq, 2, n_kv, HEAD_DIM, lb), lambda i: (i, 0, 0, 0, 0)),
                  pl.BlockSpec((nseq, 1, 2 * width), lambda i: (i, 0, g)),
                  full2((lb, n_keys)), full2((hd, width)), full2((width, hd)), full2((n_heads, hd))],
        out_specs=[pl.BlockSpec((nseq, 1, hd), lambda i: (i, 0, 0)),
                   pl.BlockSpec((nseq, 1, LANES), lambda i: (i, 0, 0))],
        out_shape=[jax.ShapeDtypeStruct((db, 1, hd), F32), jax.ShapeDtypeStruct((db, 1, LANES), F32)],
        compiler_params=_params(("parallel",)),
        name=f"dil_sample_{g}",
    )(q3, buf_t, kv3, pick, fold, unfold, hmask)
    return o.reshape(db, hd), lse.reshape(db, LANES)


def _route(h, wr_ref, br_ref, ltri_ref, carry_ref):
    tm = h.shape[0]
    h3 = _split3(h)
    w3 = _split3(wr_ref[...])
    logits = br_ref[...]
    for a, b in ((0, 0), (0, 1), (1, 0), (1, 1), (0, 2), (2, 0)):
        logits = logits + _dot(h3[a], w3[b])
    lane = lax.broadcasted_iota(I32, (tm, LANES), 1)
    lane_f = lane.astype(F32)
    big = float(LANES)
    is_group = (lane >= GROUP_LANE0) & (lane < GROUP_LANE0 + N_GROUPS)
    gl = jnp.where(is_group, logits, NEG_INF)
    gmax = jnp.max(gl, axis=-1, keepdims=True)
    gate = 1.0 / jnp.sum(jnp.exp(gl - gmax), axis=-1, keepdims=True)
    gidx = jnp.min(jnp.where(gl == gmax, lane_f, big), axis=-1, keepdims=True) - GROUP_LANE0
    lo = gidx * EXPERTS_PER_GROUP
    in_group = (lane_f >= lo) & (lane_f < lo + EXPERTS_PER_GROUP)
    el = jnp.where(in_group, logits, NEG_INF)
    m1 = jnp.max(el, axis=-1, keepdims=True)
    i1 = jnp.min(jnp.where(el == m1, lane_f, big), axis=-1, keepdims=True)
    el2 = jnp.where(lane_f == i1, NEG_INF, el)
    m2 = jnp.max(el2, axis=-1, keepdims=True)
    i2 = jnp.min(jnp.where(el2 == m2, lane_f, big), axis=-1, keepdims=True)
    t = jnp.exp(m2 - m1)
    w1 = gate / (1.0 + t)
    w2 = gate * t / (1.0 + t)
    hit1 = lane_f == i1
    hit2 = lane_f == i2
    onehot = (hit1 | hit2).astype(F32)
    ranks = _dot(ltri_ref[...], onehot.astype(BF16)) + carry_ref[0:1, :]
    r1 = jnp.sum(jnp.where(hit1, ranks, 0.0), axis=-1, keepdims=True)
    r2 = jnp.sum(jnp.where(hit2, ranks, 0.0), axis=-1, keepdims=True)
    carry_ref[...] += jnp.sum(onehot, axis=0, keepdims=True)
    ints = (jnp.where(lane == 0, i1, 0.0) + jnp.where(lane == 1, i2, 0.0)
            + jnp.where(lane == 2, r1, 0.0) + jnp.where(lane == 3, r2, 0.0))
    wts = jnp.where(lane == 0, w1, 0.0) + jnp.where(lane == 1, w2, 0.0)
    return ints, wts


def _oproj_router_kernel(*refs, combine):
    if combine:
        (x_ref, o1_ref, o2_ref, o3_ref, l1_ref, l2_ref, l3_ref, exp_ref, wo_ref, g_ref, wr_ref, br_ref,
         ltri_ref, y_ref, h_ref, rw_ref, ri_ref, cnt_ref, carry_ref) = refs
        l1, l2, l3 = l1_ref[...], l2_ref[...], l3_ref[...]
        m = jnp.maximum(jnp.maximum(l1, l2), l3)
        e1, e2, e3 = jnp.exp(l1 - m), jnp.exp(l2 - m), jnp.exp(l3 - m)
        inv = 1.0 / (e1 + e2 + e3)
        ex = exp_ref[...]
        o = jnp.zeros(o1_ref.shape, F32)
        for e, o_ref in ((e1, o1_ref), (e2, o2_ref), (e3, o3_ref)):
            hi, lo = _split2(e * inv)
            o = o + (_dot(hi, ex) + _dot(lo, ex)) * o_ref[...]
    else:
        (x_ref, o_ref, wo_ref, g_ref, wr_ref, br_ref, ltri_ref,
         y_ref, h_ref, rw_ref, ri_ref, cnt_ref, carry_ref) = refs
        o = o_ref[...]

    @pl.when(pl.program_id(0) == 0)
    def _():
        carry_ref[...] = jnp.zeros_like(carry_ref)

    y = x_ref[...] + _dot(o.astype(BF16), wo_ref[...])
    y_ref[...] = y
    h = _rms_scale(y) * g_ref[...]
    h_ref[...] = h
    ints, wts = _route(h, wr_ref, br_ref, ltri_ref, carry_ref)
    rw_ref[...] = wts
    ri_ref[...] = ints.T[0:SUBLANES, :].astype(I32)
    cnt_ref[...] = carry_ref[...]


def _oproj_router(x, attn, wo_bf, g_ffn, w_rg, b_rg, w_re, b_re, tm):
    t, d = x.shape
    hd = wo_bf.shape[0]
    nt = t // tm
    wr = jnp.zeros((d, LANES), F32).at[:, :N_EXPERTS].set(w_re).at[:, GROUP_LANE0:GROUP_LANE0 + N_GROUPS].set(w_rg)
    br = jnp.zeros((1, LANES), F32).at[0, :N_EXPERTS].set(b_re).at[0, GROUP_LANE0:GROUP_LANE0 + N_GROUPS].set(b_rg)
    ltri = jnp.asarray(np.arange(tm)[:, None] > np.arange(tm)[None, :], BF16)
    row = lambda n: pl.BlockSpec((tm, n), lambda i: (i, 0))
    full = lambda shape: pl.BlockSpec(shape, lambda i: (0, 0))
    combine = len(attn) > 1
    if combine:
        ex = jnp.asarray(np.arange(LANES)[:, None] == np.arange(hd)[None, :] // HEAD_DIM, BF16)
        attn_specs = [row(hd)] * 3 + [row(LANES)] * 3 + [full((LANES, hd))]
        attn_args = list(attn) + [ex]
    else:
        attn_specs = [row(hd)]
        attn_args = list(attn)
    y, h, rw, ri, cnt = pl.pallas_call(
        functools.partial(_oproj_router_kernel, combine=combine),
        grid=(nt,),
        in_specs=[row(d)] + attn_specs + [full((hd, d)), full((1, d)), full((d, LANES)), full((1, LANES)),
                                          full((tm, tm))],
        out_specs=[row(d), row(d), row(LANES),
                   pl.BlockSpec((None, SUBLANES, tm), lambda i: (i, 0, 0)),
                   full((SUBLANES, LANES))],
        out_shape=[jax.ShapeDtypeStruct((t, d), F32), jax.ShapeDtypeStruct((t, d), F32),
                   jax.ShapeDtypeStruct((t, LANES), F32),
                   jax.ShapeDtypeStruct((nt, SUBLANES, tm), I32),
                   jax.ShapeDtypeStruct((SUBLANES, LANES), F32)],
        scratch_shapes=[pltpu.VMEM((SUBLANES, LANES), F32)],
        compiler_params=_params(("arbitrary",)),
        name="oproj_router_b" if combine else "oproj_router_a",
    )(x, *attn_args, wo_bf, g_ffn.reshape(1, d), wr, br, ltri)
    return y, h, rw, ri, cnt[0, :N_EXPERTS].astype(I32)


def _dispatch_kernel(offs_ref, ri_ref, h_ref, hs_ref, sem):
    tm = ri_ref.shape[-1]

    def issue(r, carry):
        src = h_ref.at[pl.ds(r, 1)]
        s1 = offs_ref[ri_ref[0, r]] + ri_ref[2, r]
        s2 = offs_ref[ri_ref[1, r]] + ri_ref[3, r]
        pltpu.make_async_copy(src, hs_ref.at[pl.ds(s1, 1)], sem).start()
        pltpu.make_async_copy(src, hs_ref.at[pl.ds(s2, 1)], sem).start()
        return carry

    lax.fori_loop(0, tm, issue, 0)

    def drain(r, carry):
        pltpu.make_async_copy(h_ref.at[pl.ds(0, 1)], hs_ref.at[pl.ds(0, 1)], sem).wait()
        pltpu.make_async_copy(h_ref.at[pl.ds(0, 1)], hs_ref.at[pl.ds(0, 1)], sem).wait()
        return carry

    lax.fori_loop(0, tm, drain, 0)


def _dispatch(h, ri, offs, n_rows):
    t, d = h.shape
    nt, _, tm = ri.shape
    grid_spec = pltpu.PrefetchScalarGridSpec(
        num_scalar_prefetch=1,
        grid=(nt,),
        in_specs=[pl.BlockSpec((None, SUBLANES, tm), lambda i, offs: (i, 0, 0), memory_space=pltpu.SMEM),
                  pl.BlockSpec((tm, d), lambda i, offs: (i, 0))],
        out_specs=pl.BlockSpec(memory_space=pl.ANY),
        scratch_shapes=[pltpu.SemaphoreType.DMA(())])
    return pl.pallas_call(
        _dispatch_kernel,
        grid_spec=grid_spec,
        out_shape=jax.ShapeDtypeStruct((n_rows, d), F32),
        compiler_params=pltpu.CompilerParams(dimension_semantics=("arbitrary",), has_side_effects=True),
        name="moe_dispatch",
    )(offs, ri, h)


def _experts_kernel(blk_ref, exp_ref, used_ref, x_ref, wg_ref, wu_ref, wd_ref, o_ref):
    i = pl.program_id(0)

    @pl.when(i < used_ref[0])
    def _():
        x = x_ref[...].astype(BF16)
        a = _dot(x, wg_ref[...])
        u = _dot(x, wu_ref[...])
        mid = a * (1.0 / (1.0 + jnp.exp(-a))) * u
        o_ref[...] = _dot(mid.astype(BF16), wd_ref[...])


def _experts(hs, wg_bf, wu_bf, wd_bf, tile_blk, tile_exp, n_used, tm):
    n_rows, d = hs.shape
    f = wg_bf.shape[-1]
    n_tiles = n_rows // tm
    grid_spec = pltpu.PrefetchScalarGridSpec(
        num_scalar_prefetch=3,
        grid=(n_tiles,),
        in_specs=[pl.BlockSpec((tm, d), lambda i, blk, ex, used: (blk[i], 0)),
                  pl.BlockSpec((None, d, f), lambda i, blk, ex, used: (ex[i], 0, 0)),
                  pl.BlockSpec((None, d, f), lambda i, blk, ex, used: (ex[i], 0, 0)),
                  pl.BlockSpec((None, f, d), lambda i, blk, ex, used: (ex[i], 0, 0))],
        out_specs=pl.BlockSpec((tm, d), lambda i, blk, ex, used: (blk[i], 0)))
    return pl.pallas_call(
        _experts_kernel,
        grid_spec=grid_spec,
        out_shape=jax.ShapeDtypeStruct((n_rows, d), F32),
        compiler_params=_params(("arbitrary",)),
        name="moe_experts",
    )(tile_blk, tile_exp, n_used, hs, wg_bf, wu_bf, wd_bf)


def _combine_kernel(offs_ref, ri_ref, y_ref, rw_ref, ys_ref, o_ref, g1, g2, sem):
    tm = y_ref.shape[0]

    def issue(r, carry):
        s1 = offs_ref[ri_ref[0, r]] + ri_ref[2, r]
        s2 = offs_ref[ri_ref[1, r]] + ri_ref[3, r]
        pltpu.make_async_copy(ys_ref.at[pl.ds(s1, 1)], g1.at[pl.ds(r, 1)], sem).start()
        pltpu.make_async_copy(ys_ref.at[pl.ds(s2, 1)], g2.at[pl.ds(r, 1)], sem).start()
        return carry

    lax.fori_loop(0, tm, issue, 0)

    def drain(r, carry):
        pltpu.make_async_copy(ys_ref.at[pl.ds(0, 1)], g1.at[pl.ds(0, 1)], sem).wait()
        pltpu.make_async_copy(ys_ref.at[pl.ds(0, 1)], g2.at[pl.ds(0, 1)], sem).wait()
        return carry

    lax.fori_loop(0, tm, drain, 0)
    rw = rw_ref[...]
    o_ref[...] = y_ref[...] + rw[:, 0:1] * g1[...] + rw[:, 1:2] * g2[...]


def _combine(y, rw, ri, offs, ys):
    t, d = y.shape
    nt, _, tm = ri.shape
    grid_spec = pltpu.PrefetchScalarGridSpec(
        num_scalar_prefetch=1,
        grid=(nt,),
        in_specs=[pl.BlockSpec((None, SUBLANES, tm), lambda i, offs: (i, 0, 0), memory_space=pltpu.SMEM),
                  pl.BlockSpec((tm, d), lambda i, offs: (i, 0)),
                  pl.BlockSpec((tm, LANES), lambda i, offs: (i, 0)),
                  pl.BlockSpec(memory_space=pl.ANY)],
        out_specs=pl.BlockSpec((tm, d), lambda i, offs: (i, 0)),
        scratch_shapes=[pltpu.VMEM((tm, d), F32), pltpu.VMEM((tm, d), F32), pltpu.SemaphoreType.DMA(())])
    return pl.pallas_call(
        _combine_kernel,
        grid_spec=grid_spec,
        out_shape=jax.ShapeDtypeStruct((t, d), F32),
        compiler_params=_params(("arbitrary",)),
        name="moe_combine",
    )(offs, ri, y, rw, ys)


def _moe(y, h, rw, ri, counts, wg_bf, wu_bf, wd_bf, tm_e):
    t = y.shape[0]
    n_tiles = 2 * t // tm_e + N_EXPERTS
    tiles_per = (counts + tm_e - 1) // tm_e
    tile_end = jnp.cumsum(tiles_per)
    offs = ((tile_end - tiles_per) * tm_e).astype(I32)
    n_used = tile_end[-1].astype(I32)
    tile_id = jnp.minimum(jnp.arange(n_tiles, dtype=I32), n_used - 1)
    tile_exp = jnp.sum(tile_id[:, None] >= tile_end[None, :], axis=1).astype(I32)
    hs = _dispatch(h, ri, offs, n_tiles * tm_e)
    ys = _experts(hs, wg_bf, wu_bf, wd_bf, tile_id, tile_exp, n_used.reshape(1), tm_e)
    return _combine(y, rw, ri, offs, ys)


def kernel(x_prompt, x_sample, cache_kv_a, cache_kv_b1, cache_kv_b2, cache_kv_b3, page_table, g_attn, g_ffn, w_qkv_a, w_o_a, sb_bias, g_kv, w_kv_b, k_norm_b, w_q_b, q_norm_b, w_o_b, w_router_group, b_router_group, w_router_expert, b_router_expert, w_gate, w_up, w_down):
    b, s, d = x_prompt.shape
    db, t, _ = x_sample.shape
    assert t == 1
    n_a, n_pool, page, _, n_heads_a, _ = cache_kv_a.shape
    depth = g_attn.shape[0]
    n_kv = cache_kv_b1.shape[3]
    n_heads_b = w_q_b.shape[-1] // (N_DIL * HEAD_DIM)
    hd_a = n_heads_a * HEAD_DIM
    past_len = page_table.shape[1] * page
    buffers = (cache_kv_b1, cache_kv_b2, cache_kv_b3)
    tm_p, tm_s = 256, db * t
    tme_p, tme_s = 256, 32

    xp = x_prompt.reshape(b * s, d)
    xs = x_sample.reshape(db * t, d)
    wg_bf, wu_bf, wd_bf = w_gate.astype(BF16), w_up.astype(BF16), w_down.astype(BF16)
    new_a_p, new_a_s = [], []
    kv_p = kv_s = None
    for layer in range(depth):
        if layer < n_a:
            i = layer
            wqkv = w_qkv_a[i].astype(BF16)
            wo = w_o_a[i].astype(BF16)
            qkv_p, kvt_p = _proj_a(xp, g_attn[layer], wqkv, tm_p, seq=s)
            (qkv_s,) = _proj_a(xs, g_attn[layer], wqkv, tm_s)
            o_p = _sb_prompt(qkv_p.reshape(b, s, 3 * hd_a), sb_bias[i], n_heads_a)
            cache_t = cache_kv_a.transpose(0, 1, 3, 4, 5, 2)
            o_s = _sb_sample(qkv_s.reshape(db, 1, 3 * hd_a), cache_t, i, page_table, sb_bias[i])
            attn_p = (o_p.reshape(b * s, hd_a),)
            attn_s = (o_s.reshape(db, hd_a),)
            new_a_p.append(kvt_p.reshape(b, 2, n_heads_a, HEAD_DIM, s).transpose(0, 4, 1, 2, 3))
            new_a_s.append(qkv_s[:, hd_a:].reshape(db, t, 2, n_heads_a, HEAD_DIM))
        else:
            j = layer - n_a
            wo = w_o_b[j].astype(BF16)
            wq = w_q_b[j].astype(BF16)
            wkv = w_kv_b.astype(BF16)
            tab_p = _rope_tables(jnp.arange(s))
            tab_s = _rope_tables(jnp.full((tm_s,), past_len))
            kv_pj, q_p = _proj_b(xp, g_kv, wkv, g_attn[layer], wq, k_norm_b, q_norm_b[j], tab_p, tm_p,
                                 n_heads_b, n_kv)
            kv_sj, q_s = _proj_b(xs, g_kv, wkv, g_attn[layer], wq, k_norm_b, q_norm_b[j], tab_s, tm_s,
                                 n_heads_b, n_kv)
            if j == 0:
                kv_p, kv_s = kv_pj, kv_sj
            outs_p, lses_p, outs_s, lses_s = [], [], [], []
            for g in range(N_DIL):
                o, l = _dil_prompt(q_p.reshape(b, s, -1), kv_p.reshape(b, s, -1), g, n_heads_b, n_kv)
                outs_p.append(o)
                lses_p.append(l)
                o, l = _dil_sample(q_s.reshape(db, 1, -1), buffers[g], kv_s.reshape(db, 1, -1), g, n_heads_b,
                                   n_kv)
                outs_s.append(o)
                lses_s.append(l)
            attn_p = tuple(outs_p) + tuple(lses_p)
            attn_s = tuple(outs_s) + tuple(lses_s)
        router = (g_ffn[layer], w_router_group[layer], b_router_group[layer],
                  w_router_expert[layer], b_router_expert[layer])
        experts = (wg_bf[layer], wu_bf[layer], wd_bf[layer])
        xp = _moe(*_oproj_router(xp, attn_p, wo, *router, tm_p), *experts, tme_p)
        xs = _moe(*_oproj_router(xs, attn_s, wo, *router, tm_s), *experts, tme_s)

    kv_p5 = kv_p.reshape(b, s, N_DIL, 2, n_kv, HEAD_DIM)
    kv_s5 = kv_s.reshape(db, t, N_DIL, 2, n_kv, HEAD_DIM)
    outs = [xp.reshape(b, s, d), xs.reshape(db, t, d), jnp.stack(new_a_p), jnp.stack(new_a_s)]
    for g, (window, _) in enumerate(DILATED):
        outs.append(kv_p5[:, s - min(window, s):, g])
        outs.append(kv_s5[:, :, g])
    return tuple(outs)
```

```python
import functools

import numpy as np
import jax
import jax.numpy as jnp
from jax import lax
from jax.experimental import pallas as pl
from jax.experimental.pallas import tpu as pltpu

F32 = jnp.float32
BF16 = jnp.bfloat16
I32 = jnp.int32

HEAD_DIM = 64
ROT_DIM = HEAD_DIM // 4
ROPE_THETA = 500000.0
DILATED = ((128, 1), (512, 4), (2048, 16))
N_DIL = len(DILATED)
BLOCK = 128
N_GROUPS = 4
EXPERTS_PER_GROUP = 8
N_EXPERTS = N_GROUPS * EXPERTS_PER_GROUP
RMS_EPS = 1e-6
NEG_INF = -1e30

LANES = 128
SUBLANES = 8
VMEM_LIMIT = 48 * 1024 * 1024
GROUP_LANE0 = N_EXPERTS


def _dot(a, b):
    return jnp.dot(a, b, preferred_element_type=F32)


def _dot_nt(a, b):
    return lax.dot_general(a, b, (((1,), (1,)), ((), ())), preferred_element_type=F32)


def _split2(x):
    hi = x.astype(BF16)
    lo = (x - hi.astype(F32)).astype(BF16)
    return hi, lo


def _split3(x):
    hi = x.astype(BF16)
    r = x - hi.astype(F32)
    mid = r.astype(BF16)
    lo = (r - mid.astype(F32)).astype(BF16)
    return hi, mid, lo


def _rms_scale(x):
    return x * lax.rsqrt(jnp.mean(x * x, axis=-1, keepdims=True) + RMS_EPS)


def _params(sem):
    return pltpu.CompilerParams(dimension_semantics=sem, vmem_limit_bytes=VMEM_LIMIT)


def _proj_a_kernel(x_ref, g_ref, w_ref, o_ref, *maybe_kvt_ref, tn):
    xn = (_rms_scale(x_ref[...]) * g_ref[...]).astype(BF16)
    n = o_ref.shape[-1]
    for c in range(n // tn):
        y = _dot(xn, w_ref[:, c * tn:(c + 1) * tn])
        o_ref[:, c * tn:(c + 1) * tn] = y
        if maybe_kvt_ref and c * tn >= n // 3:
            row = c * tn - n // 3
            maybe_kvt_ref[0][row:row + tn, :] = y.T


def _proj_a(x, g, w_bf, tm, seq=None):
    t, d = x.shape
    n = w_bf.shape[1]
    out_specs = [pl.BlockSpec((tm, n), lambda i: (i, 0))]
    out_shape = [jax.ShapeDtypeStruct((t, n), F32)]
    if seq is not None:
        per = seq // tm
        out_specs.append(pl.BlockSpec((None, 2 * n // 3, tm), lambda i: (i // per, 0, i % per)))
        out_shape.append(jax.ShapeDtypeStruct((t // seq, 2 * n // 3, seq), F32))
    return pl.pallas_call(
        functools.partial(_proj_a_kernel, tn=512),
        grid=(t // tm,),
        in_specs=[pl.BlockSpec((tm, d), lambda i: (i, 0)),
                  pl.BlockSpec((1, d), lambda i: (0, 0)),
                  pl.BlockSpec((d, n), lambda i: (0, 0))],
        out_specs=out_specs,
        out_shape=out_shape,
        compiler_params=_params(("parallel",)),
        name="proj_a",
    )(x, g.reshape(1, d), w_bf)


def _headnorm_rope(y, gain, bd, c, s1, s2):
    hi, lo = _split2(y * y)
    ms = (_dot(hi, bd) + _dot(lo, bd)) * (1.0 / HEAD_DIM)
    yn = y * lax.rsqrt(ms + RMS_EPS) * gain
    half = ROT_DIM // 2
    return yn * c + pltpu.roll(yn, LANES - half, 1) * s1 + pltpu.roll(yn, half, 1) * s2


def _proj_b_kernel(x_ref, gkv_ref, wkv_ref, gq_ref, wq_ref, gainkv_ref, gainq_ref, bd_ref,
                   c_ref, s1_ref, s2_ref, kv_ref, q_ref, *, tn):
    xh = _rms_scale(x_ref[...])
    bd = bd_ref[...]
    c, s1, s2 = c_ref[...], s1_ref[...], s2_ref[...]
    kv_cols = 2 * (kv_ref.shape[-1] // (2 * N_DIL))

    def run(g_ref, w_ref, gain_ref, o_ref, is_key):
        xn = (xh * g_ref[...]).astype(BF16)
        for cc in range(o_ref.shape[-1] // tn):
            y = _dot(xn, w_ref[:, cc * tn:(cc + 1) * tn])
            for s in range(tn // LANES):
                col = cc * tn + s * LANES
                ys = y[:, s * LANES:(s + 1) * LANES]
                if is_key(col):
                    ys = _headnorm_rope(ys, gain_ref[:, col:col + LANES], bd, c, s1, s2)
                o_ref[:, col:col + LANES] = ys

    run(gkv_ref, wkv_ref, gainkv_ref, kv_ref, lambda col: (col % kv_cols) < kv_cols // 2)
    run(gq_ref, wq_ref, gainq_ref, q_ref, lambda col: True)


def _rope_tables(pos):
    half = ROT_DIM // 2
    inv = jnp.power(ROPE_THETA, -jnp.arange(half, dtype=F32) * 2.0 / ROT_DIM)
    ang = pos.astype(F32)[:, None] * inv[None, :]
    cos, sin = jnp.cos(ang), jnp.sin(ang)
    n = pos.shape[0]
    ones = jnp.ones((n, HEAD_DIM - ROT_DIM), F32)
    zeros = jnp.zeros((n, HEAD_DIM - ROT_DIM), F32)
    zh = jnp.zeros((n, half), F32)
    c = jnp.concatenate([cos, cos, ones], axis=1)
    s1 = jnp.concatenate([-sin, zh, zeros], axis=1)
    s2 = jnp.concatenate([zh, sin, zeros], axis=1)
    rep = LANES // HEAD_DIM
    return tuple(jnp.tile(a, (1, rep)) for a in (c, s1, s2))


def _proj_b(x, g_kv, wkv_bf, g_q, wq_bf, k_norm, q_norm, tables, tm, n_heads_q, n_kv):
    t, d = x.shape
    nkv, nq = wkv_bf.shape[1], wq_bf.shape[1]
    gain_kv = jnp.broadcast_to(k_norm[:, None, None, :], (N_DIL, 2, n_kv, HEAD_DIM)).reshape(1, nkv)
    gain_q = jnp.broadcast_to(q_norm[:, None, :], (N_DIL, n_heads_q, HEAD_DIM)).reshape(1, nq)
    lane = np.arange(LANES)
    bd = jnp.asarray((lane[:, None] // HEAD_DIM == lane[None, :] // HEAD_DIM), BF16)
    n_tab = tables[0].shape[0] // tm
    tab_spec = pl.BlockSpec((tm, LANES), lambda i: (i % n_tab, 0))
    full = lambda shape: pl.BlockSpec(shape, lambda i: (0, 0))
    return pl.pallas_call(
        functools.partial(_proj_b_kernel, tn=512),
        grid=(t // tm,),
        in_specs=[pl.BlockSpec((tm, d), lambda i: (i, 0)),
                  full((1, d)), full((d, nkv)), full((1, d)), full((d, nq)),
                  full((1, nkv)), full((1, nq)), full((LANES, LANES)),
                  tab_spec, tab_spec, tab_spec],
        out_specs=[pl.BlockSpec((tm, nkv), lambda i: (i, 0)),
                   pl.BlockSpec((tm, nq), lambda i: (i, 0))],
        out_shape=[jax.ShapeDtypeStruct((t, nkv), F32), jax.ShapeDtypeStruct((t, nq), F32)],
        compiler_params=_params(("parallel",)),
        name="proj_b",
    )(x, g_kv.reshape(1, d), wkv_bf, g_q.reshape(1, d), wq_bf, gain_kv, gain_q, bd, *tables)


def _tree_prod(xs):
    while len(xs) > 1:
        xs = [xs[i] * xs[i + 1] for i in range(0, len(xs) - 1, 2)] + ([xs[-1]] if len(xs) % 2 else [])
    return xs[0]


def _sb_prompt_kernel(bias_ref, pm_ref, q_ref, k_ref, v_ref, o_ref, kbf, vtb, acc, zbuf, abuf, *, blk,
                      scan_blocks):
    hp = pl.program_id(1)
    qi = pl.program_id(2)
    seg = blk // SUBLANES
    per = seg // scan_blocks
    wide = 2 * blk
    n_blocks = kbf.shape[0]

    @pl.when(qi == 0)
    def _():
        pm = pm_ref[...]

        def prep(j, carry):
            rows = pl.ds(pl.multiple_of(j * blk, blk), blk)
            kbf[j] = _dot(pm, k_ref[rows, :].astype(BF16)).astype(BF16)
            vtb[j] = _dot(pm, v_ref[rows, :].astype(BF16)).T.astype(BF16)
            return carry

        lax.fori_loop(0, n_blocks, prep, 0)

    q = q_ref[...] * (HEAD_DIM ** -0.5)
    lane = lax.broadcasted_iota(I32, q.shape, 1)
    q2 = jnp.concatenate([jnp.where(lane < HEAD_DIM, q, 0.0), jnp.where(lane >= HEAD_DIM, q, 0.0)],
                         axis=0).astype(BF16)
    lane_w = lax.broadcasted_iota(I32, (1, wide), 1)
    bias = jnp.where(lane_w < blk, bias_ref[2 * hp], bias_ref[2 * hp + 1])

    sub = lax.broadcasted_iota(I32, (SUBLANES, wide), 0)

    def scores(j):
        return _dot_nt(kbf[j], q2) + bias

    def gates(z, masked):
        e = jnp.exp(-jnp.abs(z))
        r = 1.0 / (1.0 + e)
        s = e * r
        pos = z >= 0.0
        beta = jnp.where(pos, r, s)
        om = jnp.where(pos, s, r)
        if masked:
            rho = lax.broadcasted_iota(I32, (blk, wide), 0)
            key_idx = (rho & (SUBLANES - 1)) * seg + (rho >> 3)
            query_idx = lax.broadcasted_iota(I32, (blk, wide), 1) & (blk - 1)
            diag = key_idx < query_idx
            beta = jnp.where(diag, beta, 0.0)
            om = jnp.where(diag, om, 1.0)
        return beta, om

    def scan(beta, om, carry):
        oms = [om[v * SUBLANES:(v + 1) * SUBLANES, :] for v in range(seg)]
        chain_tot = [_tree_prod(oms[c * per:(c + 1) * per]) for c in range(scan_blocks)]
        inc = _tree_prod(chain_tot)
        for d in (1, 2, 4):
            inc = inc * jnp.where(sub + d < SUBLANES, pltpu.roll(inc, SUBLANES - d, 0), 1.0)
        exc = jnp.where(sub + 1 < SUBLANES, pltpu.roll(inc, SUBLANES - 1, 0), 1.0)
        new_carry = inc[0:1, :] * carry
        start = [None] * scan_blocks
        start[scan_blocks - 1] = exc * carry
        for c in range(scan_blocks - 2, -1, -1):
            start[c] = start[c + 1] * chain_tot[c + 1]
        rows = [None] * seg
        for c in range(scan_blocks):
            run = start[c]
            for v in range((c + 1) * per - 1, c * per - 1, -1):
                rows[v] = beta[v * SUBLANES:(v + 1) * SUBLANES, :] * run
                if v > c * per:
                    run = run * oms[v]
                if v % 2 == 0:
                    abuf[v * SUBLANES:(v + 2) * SUBLANES, :] = jnp.concatenate(rows[v:v + 2], axis=0).astype(BF16)
        return new_carry

    ones = jnp.ones((SUBLANES, wide), F32)
    carry = scan(*gates(scores(qi), True), ones)
    zbuf[...] = scores(jnp.maximum(qi - 1, 0))
    acc[...] = jnp.zeros_like(acc)

    def body(t, carry):
        j = qi - 1 - t
        acc[...] += _dot(vtb[j + 1], abuf[...])
        beta, om = gates(zbuf[...], False)
        zbuf[...] = scores(jnp.maximum(j - 1, 0))
        return scan(beta, om, carry)

    lax.fori_loop(0, qi, body, carry)
    acc[...] += _dot(vtb[0], abuf[...])
    row = lax.broadcasted_iota(I32, (LANES, blk), 0)
    o_ref[...] = jnp.where(row < HEAD_DIM, acc[:, :blk], acc[:, blk:]).T


def _sb_prompt(qkv, bias, n_heads, blk=256):
    b, s, _ = qkv.shape
    n_hp = n_heads * HEAD_DIM // LANES
    n_blocks = s // blk
    seg = blk // SUBLANES
    rho = np.arange(blk)
    pm = jnp.asarray(((rho % SUBLANES) * seg + rho // SUBLANES)[:, None] == np.arange(blk)[None, :], BF16)
    return pl.pallas_call(
        functools.partial(_sb_prompt_kernel, blk=blk, scan_blocks=4),
        grid=(b, n_hp, n_blocks),
        in_specs=[pl.BlockSpec(memory_space=pltpu.SMEM),
                  pl.BlockSpec((blk, blk), lambda bi, hp, qi: (0, 0)),
                  pl.BlockSpec((None, blk, LANES), lambda bi, hp, qi: (bi, qi, hp)),
                  pl.BlockSpec((None, s, LANES), lambda bi, hp, qi: (bi, 0, n_hp + hp)),
                  pl.BlockSpec((None, s, LANES), lambda bi, hp, qi: (bi, 0, 2 * n_hp + hp))],
        out_specs=pl.BlockSpec((None, blk, LANES), lambda bi, hp, qi: (bi, qi, hp)),
        out_shape=jax.ShapeDtypeStruct((b, s, n_heads * HEAD_DIM), F32),
        scratch_shapes=[pltpu.VMEM((n_blocks, blk, LANES), BF16),
                        pltpu.VMEM((n_blocks, LANES, blk), BF16),
                        pltpu.VMEM((LANES, 2 * blk), F32),
                        pltpu.VMEM((blk, 2 * blk), F32),
                        pltpu.VMEM((blk, 2 * blk), BF16)],
        compiler_params=_params(("parallel", "parallel", "arbitrary")),
        name="sb_prompt",
    )(bias, pm, qkv, qkv, qkv)


def _softplus(x):
    return jnp.maximum(x, 0.0) + jnp.log1p(jnp.exp(-jnp.abs(x)))


def _dot3(x, w):
    return sum(_dot(part, w) for part in _split3(x))


def _sb_sample_kernel(pt_ref, q_ref, *refs, nseq):
    kv_refs = refs[:nseq]
    bias_ref, hmask_ref, ut_ref, o_ref, carry, acc = refs[nseq:]
    p = pl.program_id(1)
    _, n_heads, _, page = kv_refs[0].shape
    hd = n_heads * HEAD_DIM

    @pl.when(p == 0)
    def _():
        carry[...] = jnp.zeros_like(carry)
        acc[...] = jnp.zeros_like(acc)

    hmask = hmask_ref[...]
    z_rows, values = [], []
    for s in range(nseq):
        q_rows = (hmask * (q_ref[s] * (HEAD_DIM ** -0.5))).astype(BF16)
        k_t = kv_refs[s][0].reshape(hd, page).astype(BF16)
        values.append(kv_refs[s][1].reshape(hd, page).astype(BF16))
        z_rows.append(_dot(q_rows, k_t))
    z = jnp.concatenate(z_rows, axis=0) + bias_ref[...]
    log_rest = -_softplus(z)
    log_beta = log_rest + z
    tail = _dot3(log_rest, ut_ref[...]) + carry[...]
    a = jnp.exp(log_beta + tail).astype(BF16)
    carry[...] += jnp.sum(log_rest, axis=-1, keepdims=True)
    for s in range(nseq):
        acc[s] += _dot_nt(a[s * n_heads:(s + 1) * n_heads, :], values[s])

    @pl.when(p == pl.num_programs(1) - 1)
    def _():
        for s in range(nseq):
            o_ref[s] = jnp.sum(acc[s] * hmask, axis=0, keepdims=True)


def _sb_sample(q3, cache_t, layer, page_table, bias, nseq=8):
    db, n_pages = page_table.shape
    _, _, _, n_heads, _, page = cache_t.shape
    hd = n_heads * HEAD_DIM
    hmask = jnp.asarray(np.arange(hd)[None, :] // HEAD_DIM == np.arange(n_heads)[:, None], F32)
    ut = jnp.asarray(np.arange(page)[:, None] > np.arange(page)[None, :], BF16)
    bias2 = jnp.tile(jnp.broadcast_to(bias[:, None], (n_heads, page)), (nseq, 1))
    const = lambda shape: pl.BlockSpec(shape, lambda b, p, pt: (0, 0))
    page_spec = lambda s: pl.BlockSpec(
        (None, None, 2, n_heads, HEAD_DIM, page),
        lambda b, p, pt: (layer, pt[b * nseq + s, n_pages - 1 - p], 0, 0, 0, 0))
    grid_spec = pltpu.PrefetchScalarGridSpec(
        num_scalar_prefetch=1,
        grid=(db // nseq, n_pages),
        in_specs=[pl.BlockSpec((nseq, 1, hd), lambda b, p, pt: (b, 0, 0))]
                 + [page_spec(s) for s in range(nseq)]
                 + [const((nseq * n_heads, page)), const((n_heads, hd)), const((page, page))],
        out_specs=pl.BlockSpec((nseq, 1, hd), lambda b, p, pt: (b, 0, 0)),
        scratch_shapes=[pltpu.VMEM((nseq * n_heads, page), F32), pltpu.VMEM((nseq, n_heads, hd), F32)])
    return pl.pallas_call(
        functools.partial(_sb_sample_kernel, nseq=nseq),
        grid_spec=grid_spec,
        out_shape=jax.ShapeDtypeStruct((db, 1, hd), F32),
        compiler_params=_params(("parallel", "arbitrary")),
        name="sb_sample",
    )(page_table, q3, *([cache_t] * nseq), bias2, hmask, ut)


def _dil_prompt_kernel(q_ref, kp_ref, kc_ref, vp_ref, vc_ref, rep_ref, o_ref, lse_ref, *, steps, n_kv):
    i = pl.program_id(2)
    blk = q_ref.shape[0]
    width = kc_ref.shape[-1]
    per_kv = q_ref.shape[-1] // width
    k2 = jnp.concatenate([kp_ref[...], kc_ref[...]], axis=0).astype(BF16)
    v2 = jnp.concatenate([vp_ref[...], vc_ref[...]], axis=0).astype(BF16)
    n_heads = n_kv * per_kv
    row = lax.broadcasted_iota(I32, (n_heads * blk, 2 * blk), 0) & (blk - 1)
    col = lax.broadcasted_iota(I32, (n_heads * blk, 2 * blk), 1)
    rel = blk + row - col
    first_col = jnp.where(i > 0, 0, blk)
    mask = (rel >= 0) & (rel <= steps) & (col >= first_col)
    lane = lax.broadcasted_iota(I32, (blk, width), 1)
    sels = [(lane >= j * HEAD_DIM) & (lane < (j + 1) * HEAD_DIM) for j in range(per_kv)]
    scale = HEAD_DIM ** -0.5
    s_parts, v_reps = [], []
    for c in range(n_kv):
        rep = rep_ref[c]
        k_rep = _dot(k2, rep).astype(BF16)
        v_reps.append(_dot(v2, rep).astype(BF16))
        qc = q_ref[:, c * width:(c + 1) * width] * scale
        qs = jnp.concatenate([jnp.where(sel, qc, 0.0) for sel in sels], axis=0).astype(BF16)
        s_parts.append(_dot_nt(qs, k_rep))
    s = jnp.where(mask, jnp.concatenate(s_parts, axis=0), NEG_INF)
    m = jnp.max(s, axis=-1, keepdims=True)
    pexp = jnp.exp(s - m)
    den = jnp.sum(pexp, axis=-1, keepdims=True)
    lse = m + jnp.log(den)
    p = (pexp * (1.0 / den)).astype(BF16)
    for c in range(n_kv):
        oc = _dot(p[c * per_kv * blk:(c + 1) * per_kv * blk, :], v_reps[c])
        o_c = jnp.zeros((blk, width), F32)
        for j, sel in enumerate(sels):
            o_c = o_c + jnp.where(sel, oc[j * blk:(j + 1) * blk, :], 0.0)
        o_ref[:, c * width:(c + 1) * width] = o_c
    lane_l = lax.broadcasted_iota(I32, (blk, LANES), 1)
    lse_tile = jnp.zeros((blk, LANES), F32)
    for h in range(n_heads):
        lse_tile = lse_tile + jnp.where(lane_l == h, lse[h * blk:(h + 1) * blk, :], 0.0)
    lse_ref[...] = lse_tile


def _rep_matrices(n_kv):
    width = n_kv * HEAD_DIM
    src = np.arange(width)[:, None]
    dst = np.arange(width)[None, :]
    return jnp.asarray(np.stack([(src == c * HEAD_DIM + dst % HEAD_DIM) for c in range(n_kv)]), BF16)


def _dil_prompt(q, kv, g, n_heads, n_kv):
    b, s, nq = q.shape
    nkv = kv.shape[-1]
    window, dil = DILATED[g]
    steps = window // dil
    hd = n_heads * HEAD_DIM
    width = n_kv * HEAD_DIM
    n_sub = s // dil
    nb = n_sub // BLOCK
    qv = q.reshape(b, n_sub, dil * nq)
    kvv = kv.reshape(b, n_sub, dil * nkv)
    q_blocks = nq // hd
    kv_blocks = nkv // width
    kcol = lambda r: r * kv_blocks + g * 2
    out = pl.pallas_call(
        functools.partial(_dil_prompt_kernel, steps=steps, n_kv=n_kv),
        grid=(b, dil, nb),
        in_specs=[pl.BlockSpec((None, BLOCK, hd), lambda bi, r, i: (bi, i, r * q_blocks + g)),
                  pl.BlockSpec((None, BLOCK, width), lambda bi, r, i: (bi, jnp.maximum(i - 1, 0), kcol(r))),
                  pl.BlockSpec((None, BLOCK, width), lambda bi, r, i: (bi, i, kcol(r))),
                  pl.BlockSpec((None, BLOCK, width), lambda bi, r, i: (bi, jnp.maximum(i - 1, 0), kcol(r) + 1)),
                  pl.BlockSpec((None, BLOCK, width), lambda bi, r, i: (bi, i, kcol(r) + 1)),
                  pl.BlockSpec((n_kv, width, width), lambda bi, r, i: (0, 0, 0))],
        out_specs=[pl.BlockSpec((None, BLOCK, hd), lambda bi, r, i: (bi, i, r)),
                   pl.BlockSpec((None, BLOCK, LANES), lambda bi, r, i: (bi, i, r))],
        out_shape=[jax.ShapeDtypeStruct((b, n_sub, dil * hd), F32),
                   jax.ShapeDtypeStruct((b, n_sub, dil * LANES), F32)],
        compiler_params=_params(("parallel", "parallel", "parallel")),
        name=f"dil_prompt_{g}",
    )(qv, kvv, kvv, kvv, kvv, _rep_matrices(n_kv))
    return out[0].reshape(b * s, hd), out[1].reshape(b * s, LANES)


def _dil_sample_kernel(q_ref, buf_ref, kvn_ref, pick_ref, fold_ref, unfold_ref, hmask_ref, o_ref, lse_ref, *,
                       nseq):
    width = kvn_ref.shape[-1] // 2
    n_heads, hd = hmask_ref.shape
    lb, n_keys = pick_ref.shape
    hmask = hmask_ref[...]
    fold = fold_ref[...]
    unfold = unfold_ref[...]
    pick = pick_ref[...]
    scale = HEAD_DIM ** -0.5
    key_ok = jnp.sum(pick.astype(F32), axis=0, keepdims=True) > 0.0
    eye = (lax.broadcasted_iota(I32, (n_heads, LANES), 0) == lax.broadcasted_iota(I32, (n_heads, LANES), 1))
    for n in range(nseq):
        q_rows = (hmask * (q_ref[n] * scale)).astype(BF16)
        qm = _dot(q_rows, fold)
        k_t = _dot(buf_ref[n, 0].reshape(width, lb).astype(BF16), pick).astype(BF16)
        v_t = _dot(buf_ref[n, 1].reshape(width, lb).astype(BF16), pick).astype(BF16)
        k_new = kvn_ref[n, :, :width]
        v_new = kvn_ref[n, :, width:]
        s = jnp.where(key_ok, _dot(qm.astype(BF16), k_t), NEG_INF)
        s_new = jnp.sum(qm * k_new, axis=-1, keepdims=True)
        m = jnp.maximum(jnp.max(s, axis=-1, keepdims=True), s_new)
        den = jnp.sum(jnp.exp(s - m), axis=-1, keepdims=True) + jnp.exp(s_new - m)
        lse = m + jnp.log(den)
        p = jnp.exp(s - lse)
        p_new = jnp.exp(s_new - lse)
        o = _dot_nt(p.astype(BF16), v_t) + p_new * v_new
        o_full = _dot(o.astype(BF16), unfold) * hmask
        o_ref[n] = jnp.sum(o_full, axis=0, keepdims=True)
        lse_ref[n] = jnp.sum(jnp.where(eye, lse, 0.0), axis=0, keepdims=True)


def _dil_sample(q3, buf, kv3, g, n_heads, n_kv):
    db, lb = buf.shape[:2]
    window, dil = DILATED[g]
    steps = window // dil
    hd = n_heads * HEAD_DIM
    width = n_kv * HEAD_DIM
    n_keys = min(steps, lb // dil)
    rows_read = lb - (n_keys - np.arange(n_keys)) * dil
    pick = jnp.asarray(np.arange(lb)[:, None] == rows_read[None, :], BF16)
    buf_t = buf.transpose(0, 2, 3, 4, 1)
    nseq = max(1, min(8, (8 * 1024 * 1024) // (2 * width * lb * 4)))
    per_kv = n_heads // n_kv
    src = np.arange(hd)[:, None]
    dst = np.arange(width)[None, :]
    fold_np = (src // (HEAD_DIM * per_kv) == dst // HEAD_DIM) & (src % HEAD_DIM == dst % HEAD_DIM)
    fold = jnp.asarray(fold_np, BF16)
    unfold = jnp.asarray(fold_np.T, BF16)
    hmask = jnp.asarray(np.arange(hd)[None, :] // HEAD_DIM == np.arange(n_heads)[:, None], F32)
    full2 = lambda shape: pl.BlockSpec(shape, lambda i: (0, 0))
    o, lse = pl.pallas_call(
        functools.partial(_dil_sample_kernel, nseq=nseq),
        grid=(db // nseq,),
        in_specs=[pl.BlockSpec((nseq, 1, hd), lambda i: (i, 0, g)),
                  pl.BlockSpec((nseq, 2, n_kv, HEAD_DIM, lb), lambda i: (i, 0, 0, 0, 0)),
                  pl.BlockSpec((nseq, 1, 2 * width), lambda i: (i, 0, g)),
                  full2((lb, n_keys)), full2((hd, width)), full2((width, hd)), full2((n_heads, hd))],
        out_specs=[pl.BlockSpec((nseq, 1, hd), lambda i: (i, 0, 0)),
                   pl.BlockSpec((nseq, 1, LANES), lambda i: (i, 0, 0))],
        out_shape=[jax.ShapeDtypeStruct((db, 1, hd), F32), jax.ShapeDtypeStruct((db, 1, LANES), F32)],
        compiler_params=_params(("parallel",)),
        name=f"dil_sample_{g}",
    )(q3, buf_t, kv3, pick, fold, unfold, hmask)
    return o.reshape(db, hd), lse.reshape(db, LANES)


def _route(h, wr_ref, br_ref, ltri_ref, carry_ref):
    tm = h.shape[0]
    h3 = _split3(h)
    w3 = _split3(wr_ref[...])
    logits = br_ref[...]
    for a, b in ((0, 0), (0, 1), (1, 0), (1, 1), (0, 2), (2, 0)):
        logits = logits + _dot(h3[a], w3[b])
    lane = lax.broadcasted_iota(I32, (tm, LANES), 1)
    lane_f = lane.astype(F32)
    big = float(LANES)
    is_group = (lane >= GROUP_LANE0) & (lane < GROUP_LANE0 + N_GROUPS)
    gl = jnp.where(is_group, logits, NEG_INF)
    gmax = jnp.max(gl, axis=-1, keepdims=True)
    gate = 1.0 / jnp.sum(jnp.exp(gl - gmax), axis=-1, keepdims=True)
    gidx = jnp.min(jnp.where(gl == gmax, lane_f, big), axis=-1, keepdims=True) - GROUP_LANE0
    lo = gidx * EXPERTS_PER_GROUP
    in_group = (lane_f >= lo) & (lane_f < lo + EXPERTS_PER_GROUP)
    el = jnp.where(in_group, logits, NEG_INF)
    m1 = jnp.max(el, axis=-1, keepdims=True)
    i1 = jnp.min(jnp.where(el == m1, lane_f, big), axis=-1, keepdims=True)
    el2 = jnp.where(lane_f == i1, NEG_INF, el)
    m2 = jnp.max(el2, axis=-1, keepdims=True)
    i2 = jnp.min(jnp.where(el2 == m2, lane_f, big), axis=-1, keepdims=True)
    t = jnp.exp(m2 - m1)
    w1 = gate / (1.0 + t)
    w2 = gate * t / (1.0 + t)
    hit1 = lane_f == i1
    hit2 = lane_f == i2
    onehot = (hit1 | hit2).astype(F32)
    ranks = _dot(ltri_ref[...], onehot.astype(BF16)) + carry_ref[0:1, :]
    r1 = jnp.sum(jnp.where(hit1, ranks, 0.0), axis=-1, keepdims=True)
    r2 = jnp.sum(jnp.where(hit2, ranks, 0.0), axis=-1, keepdims=True)
    carry_ref[...] += jnp.sum(onehot, axis=0, keepdims=True)
    ints = (jnp.where(lane == 0, i1, 0.0) + jnp.where(lane == 1, i2, 0.0)
            + jnp.where(lane == 2, r1, 0.0) + jnp.where(lane == 3, r2, 0.0))
    wts = jnp.where(lane == 0, w1, 0.0) + jnp.where(lane == 1, w2, 0.0)
    return ints, wts


def _oproj_router_kernel(*refs, combine):
    if combine:
        (x_ref, o1_ref, o2_ref, o3_ref, l1_ref, l2_ref, l3_ref, exp_ref, wo_ref, g_ref, wr_ref, br_ref,
         ltri_ref, y_ref, h_ref, rw_ref, ri_ref, cnt_ref, carry_ref) = refs
        l1, l2, l3 = l1_ref[...], l2_ref[...], l3_ref[...]
        m = jnp.maximum(jnp.maximum(l1, l2), l3)
        e1, e2, e3 = jnp.exp(l1 - m), jnp.exp(l2 - m), jnp.exp(l3 - m)
        inv = 1.0 / (e1 + e2 + e3)
        ex = exp_ref[...]
        o = jnp.zeros(o1_ref.shape, F32)
        for e, o_ref in ((e1, o1_ref), (e2, o2_ref), (e3, o3_ref)):
            hi, lo = _split2(e * inv)
            o = o + (_dot(hi, ex) + _dot(lo, ex)) * o_ref[...]
    else:
        (x_ref, o_ref, wo_ref, g_ref, wr_ref, br_ref, ltri_ref,
         y_ref, h_ref, rw_ref, ri_ref, cnt_ref, carry_ref) = refs
        o = o_ref[...]

    @pl.when(pl.program_id(0) == 0)
    def _():
        carry_ref[...] = jnp.zeros_like(carry_ref)

    y = x_ref[...] + _dot(o.astype(BF16), wo_ref[...])
    y_ref[...] = y
    h = _rms_scale(y) * g_ref[...]
    h_ref[...] = h
    ints, wts = _route(h, wr_ref, br_ref, ltri_ref, carry_ref)
    rw_ref[...] = wts
    ri_ref[...] = ints.T[0:SUBLANES, :].astype(I32)
    cnt_ref[...] = carry_ref[...]


def _oproj_router(x, attn, wo_bf, g_ffn, w_rg, b_rg, w_re, b_re, tm):
    t, d = x.shape
    hd = wo_bf.shape[0]
    nt = t // tm
    wr = jnp.zeros((d, LANES), F32).at[:, :N_EXPERTS].set(w_re).at[:, GROUP_LANE0:GROUP_LANE0 + N_GROUPS].set(w_rg)
    br = jnp.zeros((1, LANES), F32).at[0, :N_EXPERTS].set(b_re).at[0, GROUP_LANE0:GROUP_LANE0 + N_GROUPS].set(b_rg)
    ltri = jnp.asarray(np.arange(tm)[:, None] > np.arange(tm)[None, :], BF16)
    row = lambda n: pl.BlockSpec((tm, n), lambda i: (i, 0))
    full = lambda shape: pl.BlockSpec(shape, lambda i: (0, 0))
    combine = len(attn) > 1
    if combine:
        ex = jnp.asarray(np.arange(LANES)[:, None] == np.arange(hd)[None, :] // HEAD_DIM, BF16)
        attn_specs = [row(hd)] * 3 + [row(LANES)] * 3 + [full((LANES, hd))]
        attn_args = list(attn) + [ex]
    else:
        attn_specs = [row(hd)]
        attn_args = list(attn)
    y, h, rw, ri, cnt = pl.pallas_call(
        functools.partial(_oproj_router_kernel, combine=combine),
        grid=(nt,),
        in_specs=[row(d)] + attn_specs + [full((hd, d)), full((1, d)), full((d, LANES)), full((1, LANES)),
                                          full((tm, tm))],
        out_specs=[row(d), row(d), row(LANES),
                   pl.BlockSpec((None, SUBLANES, tm), lambda i: (i, 0, 0)),
                   full((SUBLANES, LANES))],
        out_shape=[jax.ShapeDtypeStruct((t, d), F32), jax.ShapeDtypeStruct((t, d), F32),
                   jax.ShapeDtypeStruct((t, LANES), F32),
                   jax.ShapeDtypeStruct((nt, SUBLANES, tm), I32),
                   jax.ShapeDtypeStruct((SUBLANES, LANES), F32)],
        scratch_shapes=[pltpu.VMEM((SUBLANES, LANES), F32)],
        compiler_params=_params(("arbitrary",)),
        name="oproj_router_b" if combine else "oproj_router_a",
    )(x, *attn_args, wo_bf, g_ffn.reshape(1, d), wr, br, ltri)
    return y, h, rw, ri, cnt[0, :N_EXPERTS].astype(I32)


def _dispatch_kernel(offs_ref, ri_ref, h_ref, hs_ref, sem):
    tm = ri_ref.shape[-1]

    def issue(r, carry):
        src = h_ref.at[pl.ds(r, 1)]
        s1 = offs_ref[ri_ref[0, r]] + ri_ref[2, r]
        s2 = offs_ref[ri_ref[1, r]] + ri_ref[3, r]
        pltpu.make_async_copy(src, hs_ref.at[pl.ds(s1, 1)], sem).start()
        pltpu.make_async_copy(src, hs_ref.at[pl.ds(s2, 1)], sem).start()
        return carry

    lax.fori_loop(0, tm, issue, 0, unroll=4)
    for _ in range(2):
        pltpu.make_async_copy(h_ref, hs_ref.at[pl.ds(0, tm)], sem).wait()


def _dispatch(h, ri, offs, n_rows):
    t, d = h.shape
    nt, _, tm = ri.shape
    grid_spec = pltpu.PrefetchScalarGridSpec(
        num_scalar_prefetch=1,
        grid=(nt,),
        in_specs=[pl.BlockSpec((None, SUBLANES, tm), lambda i, offs: (i, 0, 0), memory_space=pltpu.SMEM),
                  pl.BlockSpec((tm, d), lambda i, offs: (i, 0))],
        out_specs=pl.BlockSpec(memory_space=pl.ANY),
        scratch_shapes=[pltpu.SemaphoreType.DMA(())])
    return pl.pallas_call(
        _dispatch_kernel,
        grid_spec=grid_spec,
        out_shape=jax.ShapeDtypeStruct((n_rows, d), F32),
        compiler_params=pltpu.CompilerParams(dimension_semantics=("arbitrary",), has_side_effects=True),
        name="moe_dispatch",
    )(offs, ri, h)


def _experts_kernel(blk_ref, exp_ref, used_ref, x_ref, wg_ref, wu_ref, wd_ref, o_ref):
    i = pl.program_id(0)

    @pl.when(i < used_ref[0])
    def _():
        x = x_ref[...].astype(BF16)
        a = _dot(x, wg_ref[...])
        u = _dot(x, wu_ref[...])
        mid = a * (1.0 / (1.0 + jnp.exp(-a))) * u
        o_ref[...] = _dot(mid.astype(BF16), wd_ref[...])


def _experts(hs, wg_bf, wu_bf, wd_bf, tile_blk, tile_exp, n_used, tm):
    n_rows, d = hs.shape
    f = wg_bf.shape[-1]
    n_tiles = n_rows // tm
    grid_spec = pltpu.PrefetchScalarGridSpec(
        num_scalar_prefetch=3,
        grid=(n_tiles,),
        in_specs=[pl.BlockSpec((tm, d), lambda i, blk, ex, used: (blk[i], 0)),
                  pl.BlockSpec((None, d, f), lambda i, blk, ex, used: (ex[i], 0, 0)),
                  pl.BlockSpec((None, d, f), lambda i, blk, ex, used: (ex[i], 0, 0)),
                  pl.BlockSpec((None, f, d), lambda i, blk, ex, used: (ex[i], 0, 0))],
        out_specs=pl.BlockSpec((tm, d), lambda i, blk, ex, used: (blk[i], 0)))
    return pl.pallas_call(
        _experts_kernel,
        grid_spec=grid_spec,
        out_shape=jax.ShapeDtypeStruct((n_rows, d), F32),
        compiler_params=_params(("arbitrary",)),
        name="moe_experts",
    )(tile_blk, tile_exp, n_used, hs, wg_bf, wu_bf, wd_bf)


def _combine_kernel(offs_ref, ri_ref, y_ref, rw_ref, ys_ref, o_ref, g1, g2, sem):
    tm = y_ref.shape[0]

    def issue(r, carry):
        s1 = offs_ref[ri_ref[0, r]] + ri_ref[2, r]
        s2 = offs_ref[ri_ref[1, r]] + ri_ref[3, r]
        pltpu.make_async_copy(ys_ref.at[pl.ds(s1, 1)], g1.at[pl.ds(r, 1)], sem).start()
        pltpu.make_async_copy(ys_ref.at[pl.ds(s2, 1)], g2.at[pl.ds(r, 1)], sem).start()
        return carry

    lax.fori_loop(0, tm, issue, 0, unroll=4)
    pltpu.make_async_copy(ys_ref.at[pl.ds(0, tm)], g1, sem).wait()
    pltpu.make_async_copy(ys_ref.at[pl.ds(0, tm)], g2, sem).wait()
    rw = rw_ref[...]
    o_ref[...] = y_ref[...] + rw[:, 0:1] * g1[...] + rw[:, 1:2] * g2[...]


def _combine(y, rw, ri, offs, ys):
    t, d = y.shape
    nt, _, tm = ri.shape
    grid_spec = pltpu.PrefetchScalarGridSpec(
        num_scalar_prefetch=1,
        grid=(nt,),
        in_specs=[pl.BlockSpec((None, SUBLANES, tm), lambda i, offs: (i, 0, 0), memory_space=pltpu.SMEM),
                  pl.BlockSpec((tm, d), lambda i, offs: (i, 0)),
                  pl.BlockSpec((tm, LANES), lambda i, offs: (i, 0)),
                  pl.BlockSpec(memory_space=pl.ANY)],
        out_specs=pl.BlockSpec((tm, d), lambda i, offs: (i, 0)),
        scratch_shapes=[pltpu.VMEM((tm, d), F32), pltpu.VMEM((tm, d), F32), pltpu.SemaphoreType.DMA(())])
    return pl.pallas_call(
        _combine_kernel,
        grid_spec=grid_spec,
        out_shape=jax.ShapeDtypeStruct((t, d), F32),
        compiler_params=_params(("arbitrary",)),
        name="moe_combine",
    )(offs, ri, y, rw, ys)


def _moe(y, h, rw, ri, counts, wg_bf, wu_bf, wd_bf, tm_e):
    t = y.shape[0]
    n_tiles = 2 * t // tm_e + N_EXPERTS
    tiles_per = (counts + tm_e - 1) // tm_e
    tile_end = jnp.cumsum(tiles_per)
    offs = ((tile_end - tiles_per) * tm_e).astype(I32)
    n_used = tile_end[-1].astype(I32)
    tile_id = jnp.minimum(jnp.arange(n_tiles, dtype=I32), n_used - 1)
    tile_exp = jnp.sum(tile_id[:, None] >= tile_end[None, :], axis=1).astype(I32)
    hs = _dispatch(h, ri, offs, n_tiles * tm_e)
    ys = _experts(hs, wg_bf, wu_bf, wd_bf, tile_id, tile_exp, n_used.reshape(1), tm_e)
    return _combine(y, rw, ri, offs, ys)


def kernel(x_prompt, x_sample, cache_kv_a, cache_kv_b1, cache_kv_b2, cache_kv_b3, page_table, g_attn, g_ffn, w_qkv_a, w_o_a, sb_bias, g_kv, w_kv_b, k_norm_b, w_q_b, q_norm_b, w_o_b, w_router_group, b_router_group, w_router_expert, b_router_expert, w_gate, w_up, w_down):
    b, s, d = x_prompt.shape
    db, t, _ = x_sample.shape
    assert t == 1
    n_a, n_pool, page, _, n_heads_a, _ = cache_kv_a.shape
    depth = g_attn.shape[0]
    n_kv = cache_kv_b1.shape[3]
    n_heads_b = w_q_b.shape[-1] // (N_DIL * HEAD_DIM)
    hd_a = n_heads_a * HEAD_DIM
    past_len = page_table.shape[1] * page
    buffers = (cache_kv_b1, cache_kv_b2, cache_kv_b3)
    tm_p, tm_s = 256, db * t
    tme_p, tme_s = 256, 32

    xp = x_prompt.reshape(b * s, d)
    xs = x_sample.reshape(db * t, d)
    wg_bf, wu_bf, wd_bf = w_gate.astype(BF16), w_up.astype(BF16), w_down.astype(BF16)
    new_a_p, new_a_s = [], []
    kv_p = kv_s = None
    for layer in range(depth):
        if layer < n_a:
            i = layer
            wqkv = w_qkv_a[i].astype(BF16)
            wo = w_o_a[i].astype(BF16)
            qkv_p, kvt_p = _proj_a(xp, g_attn[layer], wqkv, tm_p, seq=s)
            (qkv_s,) = _proj_a(xs, g_attn[layer], wqkv, tm_s)
            o_p = _sb_prompt(qkv_p.reshape(b, s, 3 * hd_a), sb_bias[i], n_heads_a)
            cache_t = cache_kv_a.transpose(0, 1, 3, 4, 5, 2)
            o_s = _sb_sample(qkv_s.reshape(db, 1, 3 * hd_a), cache_t, i, page_table, sb_bias[i])
            attn_p = (o_p.reshape(b * s, hd_a),)
            attn_s = (o_s.reshape(db, hd_a),)
            new_a_p.append(kvt_p.reshape(b, 2, n_heads_a, HEAD_DIM, s).transpose(0, 4, 1, 2, 3))
            new_a_s.append(qkv_s[:, hd_a:].reshape(db, t, 2, n_heads_a, HEAD_DIM))
        else:
            j = layer - n_a
            wo = w_o_b[j].astype(BF16)
            wq = w_q_b[j].astype(BF16)
            wkv = w_kv_b.astype(BF16)
            tab_p = _rope_tables(jnp.arange(s))
            tab_s = _rope_tables(jnp.full((tm_s,), past_len))
            kv_pj, q_p = _proj_b(xp, g_kv, wkv, g_attn[layer], wq, k_norm_b, q_norm_b[j], tab_p, tm_p,
                                 n_heads_b, n_kv)
            kv_sj, q_s = _proj_b(xs, g_kv, wkv, g_attn[layer], wq, k_norm_b, q_norm_b[j], tab_s, tm_s,
                                 n_heads_b, n_kv)
            if j == 0:
                kv_p, kv_s = kv_pj, kv_sj
            outs_p, lses_p, outs_s, lses_s = [], [], [], []
            for g in range(N_DIL):
                o, l = _dil_prompt(q_p.reshape(b, s, -1), kv_p.reshape(b, s, -1), g, n_heads_b, n_kv)
                outs_p.append(o)
                lses_p.append(l)
                o, l = _dil_sample(q_s.reshape(db, 1, -1), buffers[g], kv_s.reshape(db, 1, -1), g, n_heads_b,
                                   n_kv)
                outs_s.append(o)
                lses_s.append(l)
            attn_p = tuple(outs_p) + tuple(lses_p)
            attn_s = tuple(outs_s) + tuple(lses_s)
        router = (g_ffn[layer], w_router_group[layer], b_router_group[layer],
                  w_router_expert[layer], b_router_expert[layer])
        experts = (wg_bf[layer], wu_bf[layer], wd_bf[layer])
        xp = _moe(*_oproj_router(xp, attn_p, wo, *router, tm_p), *experts, tme_p)
        xs = _moe(*_oproj_router(xs, attn_s, wo, *router, tm_s), *experts, tme_s)

    kv_p5 = kv_p.reshape(b, s, N_DIL, 2, n_kv, HEAD_DIM)
    kv_s5 = kv_s.reshape(db, t, N_DIL, 2, n_kv, HEAD_DIM)
    outs = [xp.reshape(b, s, d), xs.reshape(db, t, d), jnp.stack(new_a_p), jnp.stack(new_a_s)]
    for g, (window, _) in enumerate(DILATED):
        outs.append(kv_p5[:, s - min(window, s):, g])
        outs.append(kv_s5[:, :, g])
    return tuple(outs)
```
